```python
import jax, jax.numpy as jnp
from jax import lax
import numpy as np

D_MODEL = 4096
BATCH = 1
SEQ = 16384
DEPTH = 4

GRID_W = 64
CTX_LEN = 256
EPS = 1e-6
ADA_RANK = 256
N_MOD = 9
D_FF = D_MODEL
D_SSD = D_MODEL // 2
SSD_HEAD_DIM = 64
SSD_HEADS = D_SSD // SSD_HEAD_DIM
SSD_GROUPS = 4
HEADS_PER_GROUP = SSD_HEADS // SSD_GROUPS
D_STATE = 128
D_CONV = 5
SSD_CHUNK = 128
C_XBC = D_SSD + 2 * SSD_GROUPS * D_STATE
D_MLP = D_MODEL // 4
MLP_CHUNK = 128
MLP_GROUP_DIM = 128
MLP_GROUPS = D_MLP // MLP_GROUP_DIM
D_FFT = D_MODEL // 4
FFT_GROUP_DIM = 128
FFT_GROUPS = D_FFT // FFT_GROUP_DIM
N_BRANCH = 3
OFF_Z = 0
OFF_XBC = OFF_Z + D_SSD
OFF_DT = OFF_XBC + C_XBC
OFF_U = OFF_DT + 2 * SSD_HEADS
OFF_V = OFF_U + D_MLP
OFF_F = OFF_V + D_MLP
OFF_GATE = OFF_F + D_FFT
D_IN = OFF_GATE + N_BRANCH * D_MODEL

kernel_name = "hybrid_ssd_sgu_fourier_dit_trunk"


def rmsnorm(x, g):
    xf = x.astype(jnp.float32)
    y = xf * lax.rsqrt(jnp.mean(xf * xf, axis=-1, keepdims=True) + EPS)
    return (y * g.astype(jnp.float32)).astype(x.dtype)


def adaln_mod(cvec, w1, w2, b):
    m = (jax.nn.silu(cvec) @ w1) @ w2 + b
    return jnp.split(m, N_MOD, axis=-1)


def modulate(h, shift, scale):
    return h * (1.0 + scale[:, None, :]) + shift[:, None, :]


def swiglu(h, w1, w3, w2):
    return (jax.nn.silu(h @ w1) * (h @ w3)) @ w2


def dwconv_rows(u, rows, width, w, bias):
    b, l, ch = u.shape
    half = D_CONV // 2
    up = jnp.pad(u.reshape(b, rows, width, ch), ((0, 0), (0, 0), (half, half), (0, 0)))
    y = bias
    for k in range(D_CONV):
        y = y + up[:, :, k:k + width] * w[k]
    return y.reshape(b, l, ch)


def segsum(a):
    t = a.shape[-1]
    rep = jnp.broadcast_to(a[..., :, None], a.shape + (t,))
    strict = jnp.tril(jnp.ones((t, t), dtype=bool), -1)
    cs = jnp.cumsum(jnp.where(strict, rep, 0.0), axis=-2)
    return jnp.where(jnp.tril(jnp.ones((t, t), dtype=bool)), cs, -jnp.inf)


def ssd_scan(xs, dt, A, bm, cm, init_state):
    f32 = jnp.float32
    b, l, g, r, p = xs.shape
    nc = l // SSD_CHUNK
    a = (dt * A).astype(f32)
    xc = (xs * dt[..., None]).astype(f32).reshape(b, nc, SSD_CHUNK, g, r, p)
    bc = bm.astype(f32).reshape(b, nc, SSD_CHUNK, g, D_STATE)
    cc = cm.astype(f32).reshape(b, nc, SSD_CHUNK, g, D_STATE)
    ac = a.reshape(b, nc, SSD_CHUNK, g, r).transpose(0, 3, 4, 1, 2)
    a_cs = jnp.cumsum(ac, axis=-1)
    decay_in = jnp.exp(segsum(ac))
    cb = jnp.einsum("bcqgn,bcsgn->bgcqs", cc, bc)
    y_diag = jnp.einsum("bgrcqs,bcsgrp->bcqgrp", cb[:, :, None] * decay_in, xc)
    decay_to_end = jnp.exp(a_cs[..., -1:] - a_cs)
    states = jnp.einsum("bcsgn,bgrcs,bcsgrp->bcgrpn", bc, decay_to_end, xc)
    states = jnp.concatenate([init_state.astype(f32)[:, None], states], axis=1)
    chunk_a = jnp.pad(a_cs[..., -1], ((0, 0), (0, 0), (0, 0), (1, 0)))
    chunk_decay = jnp.exp(segsum(chunk_a))
    new_states = jnp.einsum("bgrzc,bcgrpn->bzgrpn", chunk_decay, states)
    y_off = jnp.einsum("bcqgn,bcgrpn,bgrcq->bcqgrp", cc, new_states[:, :-1], jnp.exp(a_cs))
    y = (y_diag + y_off).reshape(b, l, g, r, p)
    return y.astype(xs.dtype), new_states[:, -1]


def ssd_mixer(proj, rows, width, init_states, conv_w, conv_b, a_log, dt_bias, d_skip, norm_g):
    b, l, _ = proj.shape
    z = proj[..., OFF_Z:OFF_Z + D_SSD]
    xbc = jax.nn.silu(dwconv_rows(proj[..., OFF_XBC:OFF_XBC + C_XBC], rows, width, conv_w, conv_b))
    dt_raw = proj[..., OFF_DT:OFF_DT + 2 * SSD_HEADS]
    xs = xbc[..., :D_SSD].reshape(b, l, SSD_GROUPS, HEADS_PER_GROUP, SSD_HEAD_DIM)
    bm = xbc[..., D_SSD:D_SSD + SSD_GROUPS * D_STATE].reshape(b, l, SSD_GROUPS, D_STATE)
    cm = xbc[..., D_SSD + SSD_GROUPS * D_STATE:].reshape(b, l, SSD_GROUPS, D_STATE)
    dt = jax.nn.softplus(dt_raw.reshape(b, l, 2, SSD_GROUPS, HEADS_PER_GROUP)
                         + dt_bias.reshape(2, SSD_GROUPS, HEADS_PER_GROUP))
    A = -jnp.exp(a_log.astype(jnp.float32)).reshape(2, SSD_GROUPS, HEADS_PER_GROUP)
    init_f, init_b = init_states
    y_f, s_f = ssd_scan(xs, dt[:, :, 0], A[0], bm, cm, init_f)
    flip = lambda t: jnp.flip(t, axis=1)
    y_b, s_b = ssd_scan(flip(xs), flip(dt[:, :, 1]), A[1], flip(bm), flip(cm), init_b)
    y = y_f + flip(y_b) + d_skip.reshape(SSD_GROUPS, HEADS_PER_GROUP)[:, :, None] * xs
    y = y.reshape(b, l, D_SSD) * jax.nn.silu(z)
    y = rmsnorm(y.reshape(b, l, SSD_GROUPS, D_SSD // SSD_GROUPS), norm_g.reshape(SSD_GROUPS, -1))
    return y.reshape(b, l, D_SSD), (s_f, s_b)


def chunk_sgu(u, v, norm_g, w_s, b_s):
    b, l, _ = u.shape
    nc = l // MLP_CHUNK
    u = jax.nn.gelu(u).reshape(b, nc, MLP_CHUNK, MLP_GROUPS, MLP_GROUP_DIM)
    v = rmsnorm(jax.nn.gelu(v), norm_g).reshape(b, nc, MLP_CHUNK, MLP_GROUPS, MLP_GROUP_DIM)
    vm = jnp.einsum("gqs,bnsgc->bnqgc", w_s, v) + b_s.T[:, :, None]
    return (u * vm).reshape(b, l, D_MLP)


def fourier_mix(f):
    b, l, _ = f.shape
    fg = f.astype(jnp.float32).reshape(b, l, FFT_GROUPS, FFT_GROUP_DIM)
    out = jnp.fft.fft2(fg, axes=(1, 3), norm="ortho").real
    return out.astype(f.dtype).reshape(b, l, D_FFT)


def parallel_merge(proj, y_ssd, sgu_norm, sgu_w, sgu_b, w_br_ssd, w_br_mlp, w_br_fft, w_out):
    y_mlp = chunk_sgu(proj[..., OFF_U:OFF_U + D_MLP], proj[..., OFF_V:OFF_V + D_MLP], sgu_norm, sgu_w, sgu_b)
    y_fft = fourier_mix(proj[..., OFF_F:OFF_F + D_FFT])
    g_ssd, g_mlp, g_fft = jnp.split(jax.nn.sigmoid(proj[..., OFF_GATE:]), N_BRANCH, axis=-1)
    merged = g_ssd * (y_ssd @ w_br_ssd) + g_mlp * (y_mlp @ w_br_mlp) + g_fft * (y_fft @ w_br_fft)
    return merged @ w_out


def setup_inputs(seed: int = 0) -> dict:
    key = jax.random.key(seed)
    ks = jax.random.split(key, 40)
    L = DEPTH

    def nrm(i, shape, scale):
        return jax.random.normal(ks[i], shape, jnp.float32) * scale

    def gain(i, shape):
        return 1.0 + nrm(i, shape, 0.02)

    dt0 = jnp.exp(jax.random.uniform(ks[20], (L, 2, SSD_HEADS), jnp.float32,
                                     minval=float(np.log(1e-3)), maxval=float(np.log(1e-1))))
    return {
        "x": nrm(0, (BATCH, SEQ, D_MODEL), 1.0),
        "c": nrm(1, (BATCH, D_MODEL), 1.0),
        "ctx": nrm(2, (BATCH, CTX_LEN, D_MODEL), 1.0),
        "c_ctx": nrm(3, (D_MODEL,), 1.0),
        "ada_w1": nrm(4, (L, D_MODEL, ADA_RANK), D_MODEL ** -0.5),
        "ada_w2": nrm(5, (L, ADA_RANK, N_MOD * D_MODEL), 0.5 * ADA_RANK ** -0.5),
        "ada_b": nrm(6, (L, N_MOD * D_MODEL), 0.02),
        "norm_ffn1": gain(7, (L, D_MODEL)),
        "f1_w1": nrm(8, (L, D_MODEL, D_FF), D_MODEL ** -0.5),
        "f1_w3": nrm(9, (L, D_MODEL, D_FF), D_MODEL ** -0.5),
        "f1_w2": nrm(10, (L, D_FF, D_MODEL), D_FF ** -0.5),
        "norm_mix": gain(11, (L, D_MODEL)),
        "w_in": nrm(12, (L, D_MODEL, D_IN), D_MODEL ** -0.5),
        "b_in": nrm(13, (L, D_IN), 0.02),
        "conv_w": nrm(14, (L, D_CONV, C_XBC), D_CONV ** -0.5),
        "conv_b": nrm(15, (L, C_XBC), 0.02),
        "a_log": jnp.log(jax.random.uniform(ks[16], (L, 2, SSD_HEADS), jnp.float32, minval=1.0, maxval=16.0)),
        "dt_bias": dt0 + jnp.log(-jnp.expm1(-dt0)),
        "d_skip": gain(17, (L, SSD_HEADS)),
        "norm_ssd": gain(18, (L, D_SSD)),
        "sgu_norm": gain(19, (L, D_MLP)),
        "sgu_w": nrm(21, (L, MLP_GROUPS, MLP_CHUNK, MLP_CHUNK), 0.5 * MLP_CHUNK ** -0.5),
        "sgu_b": gain(22, (L, MLP_GROUPS, MLP_CHUNK)),
        "w_br_ssd": nrm(23, (L, D_SSD, D_MODEL), D_SSD ** -0.5),
        "w_br_mlp": nrm(24, (L, D_MLP, D_MODEL), D_MLP ** -0.5),
        "w_br_fft": nrm(25, (L, D_FFT, D_MODEL), D_FFT ** -0.5),
        "w_out": nrm(26, (L, D_MODEL, D_MODEL), D_MODEL ** -0.5),
        "norm_ffn2": gain(27, (L, D_MODEL)),
        "f2_w1": nrm(28, (L, D_MODEL, D_FF), D_MODEL ** -0.5),
        "f2_w3": nrm(29, (L, D_MODEL, D_FF), D_MODEL ** -0.5),
        "f2_w2": nrm(30, (L, D_FF, D_MODEL), D_FF ** -0.5),
        "norm_final": gain(31, (D_MODEL,)),
    }


def reference(x, c, ctx, c_ctx, ada_w1, ada_w2, ada_b, norm_ffn1, f1_w1, f1_w3, f1_w2, norm_mix, w_in, b_in,
              conv_w, conv_b, a_log, dt_bias, d_skip, norm_ssd, sgu_norm, sgu_w, sgu_b,
              w_br_ssd, w_br_mlp, w_br_fft, w_out, norm_ffn2, f2_w1, f2_w3, f2_w2, norm_final):
    b = x.shape[0]
    rows = x.shape[1] // GRID_W
    ctx_len = ctx.shape[1]
    cc = c_ctx[None, :]
    zero_state = jnp.zeros((b, SSD_GROUPS, HEADS_PER_GROUP, SSD_HEAD_DIM, D_STATE), jnp.float32)
    for i in range(DEPTH):
        last = i == DEPTH - 1
        sh1, sc1, g1, shm, scm, gm, sh2, sc2, g2 = adaln_mod(c, ada_w1[i], ada_w2[i], ada_b[i])
        csh1, csc1, cg1, cshm, cscm, cgm, csh2, csc2, cg2 = adaln_mod(cc, ada_w1[i], ada_w2[i], ada_b[i])
        x = x + 0.5 * g1[:, None] * swiglu(modulate(rmsnorm(x, norm_ffn1[i]), sh1, sc1), f1_w1[i], f1_w3[i], f1_w2[i])
        ctx = ctx + 0.5 * cg1[:, None] * swiglu(modulate(rmsnorm(ctx, norm_ffn1[i]), csh1, csc1),
                                                f1_w1[i], f1_w3[i], f1_w2[i])
        px = modulate(rmsnorm(x, norm_mix[i]), shm, scm) @ w_in[i] + b_in[i]
        pc = modulate(rmsnorm(ctx, norm_mix[i]), cshm, cscm) @ w_in[i] + b_in[i]
        ssd_params = (conv_w[i], conv_b[i], a_log[i], dt_bias[i], d_skip[i], norm_ssd[i])
        y_ssd_c, ctx_states = ssd_mixer(pc, 1, ctx_len, (zero_state, zero_state), *ssd_params)
        y_ssd_x, _ = ssd_mixer(px, rows, GRID_W, ctx_states, *ssd_params)
        merge_params = (sgu_norm[i], sgu_w[i], sgu_b[i], w_br_ssd[i], w_br_mlp[i], w_br_fft[i], w_out[i])
        x = x + gm[:, None] * parallel_merge(px, y_ssd_x, *merge_params)
        if not last:
            ctx = ctx + cgm[:, None] * parallel_merge(pc, y_ssd_c, *merge_params)
            ctx = ctx + 0.5 * cg2[:, None] * swiglu(modulate(rmsnorm(ctx, norm_ffn2[i]), csh2, csc2),
                                                    f2_w1[i], f2_w3[i], f2_w2[i])
        x = x + 0.5 * g2[:, None] * swiglu(modulate(rmsnorm(x, norm_ffn2[i]), sh2, sc2), f2_w1[i], f2_w3[i], f2_w2[i])
    return rmsnorm(x, norm_final)
```

```python
import functools

import numpy as np
import jax
import jax.numpy as jnp
from jax import lax
from jax.experimental import pallas as pl
from jax.experimental.pallas import tpu as pltpu

F32 = jnp.float32
BF16 = jnp.bfloat16
HI = lax.Precision.HIGHEST

EPS = 1e-6
GRID_W = 64
SSD_GROUPS = 4
SSD_HEAD_DIM = 64
CHUNK = 128
D_CONV = 5
GROUP_DIM = 128
N_MOD = 9
LANES = 128
ROW_BLOCK = 256
VMEM_LIMIT = 56 * 1024 * 1024


def _pick(n, cands):
    for c in cands:
        if n % c == 0:
            return c
    raise ValueError(f"no tile for {n} in {cands}")


def _params(*sem):
    return pltpu.CompilerParams(dimension_semantics=sem, vmem_limit_bytes=VMEM_LIMIT)


def _silu(v):
    return v * jax.nn.sigmoid(v)


def _gelu_tanh(v):
    return v * (0.5 * (1.0 + jnp.tanh(np.sqrt(2.0 / np.pi).astype(np.float32) * (v + 0.044715 * (v * v * v)))))


def _dot(a, b, precision=None):
    return jnp.dot(a, b, preferred_element_type=F32, precision=precision)


def _ada_kernel(cv_ref, w1_ref, w2_ref, b_ref, o_ref):
    t = _dot(_silu(cv_ref[...]), w1_ref[0], HI)
    o_ref[0] = _dot(t, w2_ref[0], HI) + b_ref[0]


def ada_mods(cv, w1, w2, b):
    nl, d, r = w1.shape
    n = w2.shape[-1]
    tn = _pick(n, (4608, 4096, 2304, 2048, 1024, 512, 256, 128))
    return pl.pallas_call(
        _ada_kernel,
        grid=(nl, n // tn),
        in_specs=[pl.BlockSpec((8, d), lambda l, j: (0, 0)),
                  pl.BlockSpec((1, d, r), lambda l, j: (l, 0, 0)),
                  pl.BlockSpec((1, r, tn), lambda l, j: (l, 0, j)),
                  pl.BlockSpec((1, 1, tn), lambda l, j: (l, 0, j))],
        out_specs=pl.BlockSpec((1, 8, tn), lambda l, j: (l, 0, j)),
        out_shape=jax.ShapeDtypeStruct((nl, 8, n), F32),
        compiler_params=_params("arbitrary", "arbitrary"),
    )(cv, w1, w2, b.reshape(nl, 1, n))


def _norm_mod_kernel(x_ref, g_ref, sh_ref, sc_ref, o_ref):
    x = x_ref[...]
    y = x * lax.rsqrt(jnp.mean(x * x, axis=-1, keepdims=True) + EPS) * g_ref[...]
    o_ref[...] = (y * (1.0 + sc_ref[0]) + sh_ref[0]).astype(o_ref.dtype)


def norm_mod(xc, g, shift, scale, seq):
    t, d = xc.shape
    nx = seq // ROW_BLOCK
    sel = lambda i: (jnp.where(i >= nx, 1, 0), 0, 0)
    return pl.pallas_call(
        _norm_mod_kernel,
        grid=(t // ROW_BLOCK,),
        in_specs=[pl.BlockSpec((ROW_BLOCK, d), lambda i: (i, 0)),
                  pl.BlockSpec((1, d), lambda i: (0, 0)),
                  pl.BlockSpec((1, 1, d), sel),
                  pl.BlockSpec((1, 1, d), sel)],
        out_specs=pl.BlockSpec((ROW_BLOCK, d), lambda i: (i, 0)),
        out_shape=jax.ShapeDtypeStruct((t, d), BF16),
        compiler_params=_params("parallel"),
    )(xc, g.reshape(1, d), shift.reshape(2, 1, d), scale.reshape(2, 1, d))


def _final_norm_kernel(x_ref, g_ref, o_ref):
    x = x_ref[...]
    o_ref[...] = x * lax.rsqrt(jnp.mean(x * x, axis=-1, keepdims=True) + EPS) * g_ref[...]


def final_norm(xc, g, seq):
    d = xc.shape[1]
    return pl.pallas_call(
        _final_norm_kernel,
        grid=(seq // ROW_BLOCK,),
        in_specs=[pl.BlockSpec((ROW_BLOCK, d), lambda i: (i, 0)),
                  pl.BlockSpec((1, d), lambda i: (0, 0))],
        out_specs=pl.BlockSpec((ROW_BLOCK, d), lambda i: (i, 0)),
        out_shape=jax.ShapeDtypeStruct((seq, d), F32),
        compiler_params=_params("parallel"),
    )(xc, g.reshape(1, d))


def _mm_tiles(t, n):
    tm = _pick(t, (1280, 1024, 768, 640, 512, 384, 256))
    tn = _pick(n, (512, 384, 256, 128))
    return tm, tn


def _up_kernel(h_ref, w1_ref, w3_ref, o_ref):
    h = h_ref[...]
    a = _dot(h, w1_ref[...])
    o_ref[...] = (_silu(a) * _dot(h, w3_ref[...])).astype(o_ref.dtype)


def swiglu_up(h, w1, w3):
    t, d = h.shape
    n = w1.shape[1]
    tm, tn = _mm_tiles(t, n)
    return pl.pallas_call(
        _up_kernel,
        grid=(t // tm, n // tn),
        in_specs=[pl.BlockSpec((tm, d), lambda i, j: (i, 0)),
                  pl.BlockSpec((d, tn), lambda i, j: (0, j)),
                  pl.BlockSpec((d, tn), lambda i, j: (0, j))],
        out_specs=pl.BlockSpec((tm, tn), lambda i, j: (i, j)),
        out_shape=jax.ShapeDtypeStruct((t, n), BF16),
        compiler_params=_params("parallel", "arbitrary"),
    )(h, w1, w3)


def _down_kernel(a_ref, w_ref, res_ref, gate_ref, o_ref, *, coef, tm, seq):
    acc = _dot(a_ref[...], w_ref[...])
    rows = pl.program_id(0) * tm + lax.broadcasted_iota(jnp.int32, (tm, 1), 0)
    gate = jnp.where(rows >= seq, gate_ref[1:2, :], gate_ref[0:1, :])
    o_ref[...] = res_ref[...] + (coef * gate) * acc


def down_residual(a, w, res, gate, coef, seq):
    t, k = a.shape
    n = w.shape[1]
    tm, tn = _mm_tiles(t, n)
    return pl.pallas_call(
        functools.partial(_down_kernel, coef=coef, tm=tm, seq=seq),
        grid=(t // tm, n // tn),
        in_specs=[pl.BlockSpec((tm, k), lambda i, j: (i, 0)),
                  pl.BlockSpec((k, tn), lambda i, j: (0, j)),
                  pl.BlockSpec((tm, tn), lambda i, j: (i, j)),
                  pl.BlockSpec((2, tn), lambda i, j: (0, j))],
        out_specs=pl.BlockSpec((tm, tn), lambda i, j: (i, j)),
        out_shape=jax.ShapeDtypeStruct((t, n), F32),
        input_output_aliases={2: 0},
        compiler_params=_params("parallel", "arbitrary"),
    )(a, w, res, gate)


def _proj_kernel(h_ref, w_ref, b_ref, o_ref, *, sigmoid):
    acc = _dot(h_ref[...], w_ref[...]) + b_ref[...]
    if sigmoid:
        acc = jax.nn.sigmoid(acc)
    o_ref[...] = acc.astype(o_ref.dtype)


def proj(h, w, b, out_dtype, sigmoid=False):
    t, d = h.shape
    n = w.shape[1]
    tm, tn = _mm_tiles(t, n)
    return pl.pallas_call(
        functools.partial(_proj_kernel, sigmoid=sigmoid),
        grid=(t // tm, n // tn),
        in_specs=[pl.BlockSpec((tm, d), lambda i, j: (i, 0)),
                  pl.BlockSpec((d, tn), lambda i, j: (0, j)),
                  pl.BlockSpec((1, tn), lambda i, j: (0, j))],
        out_specs=pl.BlockSpec((tm, tn), lambda i, j: (i, j)),
        out_shape=jax.ShapeDtypeStruct((t, n), out_dtype),
        compiler_params=_params("parallel", "arbitrary"),
    )(h, w, b.reshape(1, n))


def _merge_kernel(ys_ref, ym_ref, yf_ref, ws_ref, wm_ref, wf_ref, gs_ref, gm_ref, gf_ref, o_ref):
    acc = gs_ref[...].astype(F32) * _dot(ys_ref[...], ws_ref[...])
    acc = acc + gm_ref[...].astype(F32) * _dot(ym_ref[...], wm_ref[...])
    acc = acc + gf_ref[...].astype(F32) * _dot(yf_ref[...], wf_ref[...])
    o_ref[...] = acc.astype(o_ref.dtype)


def merge(y_ssd, y_mlp, y_fft, w_ssd, w_mlp, w_fft, gates):
    t = y_ssd.shape[0]
    n = w_ssd.shape[1]
    tm, tn = _mm_tiles(t, n)
    nb = n // tn
    row = lambda k: pl.BlockSpec((tm, k), lambda i, j: (i, 0))
    col = lambda k: pl.BlockSpec((k, tn), lambda i, j: (0, j))
    gate = lambda b: pl.BlockSpec((tm, tn), lambda i, j: (i, b * nb + j))
    return pl.pallas_call(
        _merge_kernel,
        grid=(t // tm, nb),
        in_specs=[row(y_ssd.shape[1]), row(y_mlp.shape[1]), row(y_fft.shape[1]),
                  col(w_ssd.shape[0]), col(w_mlp.shape[0]), col(w_fft.shape[0]),
                  gate(0), gate(1), gate(2)],
        out_specs=pl.BlockSpec((tm, tn), lambda i, j: (i, j)),
        out_shape=jax.ShapeDtypeStruct((t, n), BF16),
        compiler_params=_params("parallel", "arbitrary"),
    )(y_ssd, y_mlp, y_fft, w_ssd, w_mlp, w_fft, gates, gates, gates)


def _conv_kernel(u_ref, w_ref, b_ref, o_ref, *, nx, ctx_len):
    u = u_ref[...]
    rb = u.shape[0]
    width = jnp.where(pl.program_id(0) >= nx, ctx_len, GRID_W)
    pos = lax.broadcasted_iota(jnp.int32, (rb, 1), 0) & (width - 1)
    y = jnp.broadcast_to(b_ref[...], u.shape)
    for k in range(D_CONV):
        off = k - D_CONV // 2
        shifted = u if off == 0 else pltpu.roll(u, (-off) % rb, 0)
        valid = (pos + off >= 0) & (pos + off < width)
        y = y + jnp.where(valid, shifted, 0.0) * w_ref[k:k + 1, :]
    o_ref[...] = _silu(y)


def conv_silu(zx, col0, conv_w, conv_b, seq, ctx_len):
    t = zx.shape[0]
    c = conv_w.shape[1]
    tc = _pick(c, (512, 256, 128))
    assert col0 % tc == 0 and ctx_len == ROW_BLOCK and ROW_BLOCK % GRID_W == 0
    assert GRID_W & (GRID_W - 1) == 0 and ctx_len & (ctx_len - 1) == 0
    nx = seq // ROW_BLOCK
    return pl.pallas_call(
        functools.partial(_conv_kernel, nx=nx, ctx_len=ctx_len),
        grid=(t // ROW_BLOCK, c // tc),
        in_specs=[pl.BlockSpec((ROW_BLOCK, tc), lambda i, j: (i, col0 // tc + j)),
                  pl.BlockSpec((D_CONV, tc), lambda i, j: (0, j)),
                  pl.BlockSpec((1, tc), lambda i, j: (0, j))],
        out_specs=pl.BlockSpec((ROW_BLOCK, tc), lambda i, j: (i, j)),
        out_shape=jax.ShapeDtypeStruct((t, c), F32),
        compiler_params=_params("parallel", "arbitrary"),
    )(zx, conv_w, conv_b.reshape(1, c))


def _dtprep_kernel(raw_ref, bias_ref, a_ref, p_ref, dt_o, cs_o, tot_o, csrow_o, *, heads, ndg):
    v = raw_ref[...] + bias_ref[...]
    dt = jnp.maximum(v, 0.0) + jnp.log1p(jnp.exp(-jnp.abs(v)))
    a = dt * a_ref[...]
    qi = lax.broadcasted_iota(jnp.int32, (CHUNK, CHUNK), 0)
    ki = lax.broadcasted_iota(jnp.int32, (CHUNK, CHUNK), 1)
    pre = _dot((ki <= qi).astype(F32), a, HI)
    suf = _dot((ki >= qi).astype(F32), a, HI)
    tot = _dot(jnp.ones((CHUNK, CHUNK), F32), a, HI)
    lane = lax.broadcasted_iota(jnp.int32, (CHUNK, LANES), 1)
    cs = jnp.where(lane < heads, pre, suf)
    for dg in range(ndg):
        sel = p_ref[dg]
        col = _dot(cs, sel, HI)
        dt_o[dg] = _dot(dt, sel, HI)
        cs_o[dg] = col
        tot_o[dg] = _dot(tot, sel, HI)
        csrow_o[dg, 0] = col.T[0:8, :]


def dt_prep(dt_raw, dt_bias, a_neg, sel):
    t = dt_raw.shape[0]
    ndg = sel.shape[0]
    nc = t // CHUNK
    heads = dt_bias.shape[0] // 2
    pad = lambda v: jnp.pad(v.reshape(1, -1), ((0, 0), (0, LANES - v.shape[0])))
    big = jax.ShapeDtypeStruct((ndg, t, LANES), F32)
    big_spec = pl.BlockSpec((ndg, CHUNK, LANES), lambda c: (0, c, 0))
    return pl.pallas_call(
        functools.partial(_dtprep_kernel, heads=heads, ndg=ndg),
        grid=(nc,),
        in_specs=[pl.BlockSpec((CHUNK, LANES), lambda c: (c, 0)),
                  pl.BlockSpec((1, LANES), lambda c: (0, 0)),
                  pl.BlockSpec((1, LANES), lambda c: (0, 0)),
                  pl.BlockSpec((ndg, LANES, LANES), lambda c: (0, 0, 0))],
        out_specs=[big_spec, big_spec, big_spec,
                   pl.BlockSpec((ndg, 1, 8, CHUNK), lambda c: (0, c, 0, 0))],
        out_shape=[big, big, big, jax.ShapeDtypeStruct((ndg, nc, 8, CHUNK), F32)],
        compiler_params=_params("parallel"),
    )(dt_raw, pad(dt_bias), pad(a_neg), sel)


def _ssd_kernel(xs_ref, b_ref, c_ref, dt_ref, cs_ref, tot_ref, csrow_ref, e_ref, y_ref, state_ref, *, r_heads):
    d = pl.program_id(0) // SSD_GROUPS

    @pl.when(pl.program_id(1) == 0)
    def _():
        state_ref[...] = jnp.zeros_like(state_ref)

    e = e_ref[...]
    cs = cs_ref[0]
    tot = tot_ref[0]
    xdt = xs_ref[...] * _dot(dt_ref[0], e, HI)
    bm = b_ref[...]
    cm = c_ref[...].astype(BF16)
    cb = lax.dot_general(cm, bm.astype(BF16), (((1,), (1,)), ((), ())), preferred_element_type=F32)
    state = state_ref[...]
    y = _dot(cm, state.astype(BF16)) * _dot(jnp.exp(cs), e, HI)

    qi = lax.broadcasted_iota(jnp.int32, (CHUNK, CHUNK), 0)
    si = lax.broadcasted_iota(jnp.int32, (CHUNK, CHUNK), 1)
    causal = (qi - si) * (1 - 2 * d) >= 0
    lo = lax.broadcasted_iota(jnp.int32, (CHUNK, LANES), 1) < SSD_HEAD_DIM
    csrow = csrow_ref[0, 0]
    parts = []
    for k in range(r_heads // 2):
        xd = xdt[:, k * LANES:(k + 1) * LANES]
        acc = None
        for half in range(2):
            r = 2 * k + half
            diff = cs[:, r:r + 1] - csrow[r:r + 1, :]
            m = (cb * jnp.exp(jnp.where(causal, diff, -1e30))).astype(BF16)
            xh = jnp.where(lo, xd, 0.0) if half == 0 else jnp.where(lo, 0.0, xd)
            p = _dot(m, xh.astype(BF16))
            acc = p if acc is None else acc + p
        parts.append(acc)
    y_ref[0] = y + (parts[0] if len(parts) == 1 else jnp.concatenate(parts, axis=1))

    xdtw = xdt * _dot(jnp.exp(tot - cs), e, HI)
    decay = _dot(jnp.exp(tot[0:8, :]), e, HI)[0:1, :]
    state_ref[...] = state * decay + _dot(bm.T.astype(BF16), xdtw.astype(BF16))


def ssd_scan(xbc, dt, cs, tot, csrow, expand, seq, d_ssd, d_state):
    t = xbc.shape[0]
    nc = t // CHUNK
    nxc = seq // CHUNK
    ncc = nc - nxc
    gp = d_ssd // SSD_GROUPS
    r_heads = gp // SSD_HEAD_DIM
    assert d_state == LANES and r_heads % 2 == 0 and r_heads <= 8 and gp % LANES == 0

    def chunk(dg, j):
        fwd = jnp.where(j < ncc, nxc + j, j - ncc)
        bwd = jnp.where(j < ncc, nc - 1 - j, nc - 1 - j)
        return jnp.where(dg < SSD_GROUPS, fwd, bwd)

    g_of = lambda dg: dg % SSD_GROUPS
    b0 = d_ssd // LANES
    c0 = b0 + SSD_GROUPS
    dg_spec = pl.BlockSpec((1, CHUNK, LANES), lambda dg, j: (dg, chunk(dg, j), 0))
    return pl.pallas_call(
        functools.partial(_ssd_kernel, r_heads=r_heads),
        grid=(2 * SSD_GROUPS, nc),
        in_specs=[pl.BlockSpec((CHUNK, gp), lambda dg, j: (chunk(dg, j), g_of(dg))),
                  pl.BlockSpec((CHUNK, LANES), lambda dg, j: (chunk(dg, j), b0 + g_of(dg))),
                  pl.BlockSpec((CHUNK, LANES), lambda dg, j: (chunk(dg, j), c0 + g_of(dg))),
                  dg_spec, dg_spec, dg_spec,
                  pl.BlockSpec((1, 1, 8, CHUNK), lambda dg, j: (dg, chunk(dg, j), 0, 0)),
                  pl.BlockSpec((LANES, gp), lambda dg, j: (0, 0))],
        out_specs=pl.BlockSpec((1, CHUNK, gp), lambda dg, j: (dg // SSD_GROUPS, chunk(dg, j), g_of(dg))),
        out_shape=jax.ShapeDtypeStruct((2, t, d_ssd), F32),
        scratch_shapes=[pltpu.VMEM((d_state, gp), F32)],
        compiler_params=_params("arbitrary", "arbitrary"),
    )(xbc, xbc, xbc, dt, cs, tot, csrow, expand)


def _ssd_out_kernel(yf_ref, yb_ref, xs_ref, z_ref, dsk_ref, g_ref, o_ref):
    y = (yf_ref[0] + yb_ref[0] + dsk_ref[...] * xs_ref[...]) * _silu(z_ref[...])
    y = y * lax.rsqrt(jnp.mean(y * y, axis=-1, keepdims=True) + EPS) * g_ref[...]
    o_ref[...] = y.astype(o_ref.dtype)


def ssd_out(y2, xbc, zx, d_skip_x, norm_g):
    _, t, d_ssd = y2.shape
    gp = d_ssd // SSD_GROUPS
    blk = lambda: pl.BlockSpec((ROW_BLOCK, gp), lambda i, g: (i, g))
    vec = lambda: pl.BlockSpec((1, gp), lambda i, g: (0, g))
    return pl.pallas_call(
        _ssd_out_kernel,
        grid=(t // ROW_BLOCK, SSD_GROUPS),
        in_specs=[pl.BlockSpec((1, ROW_BLOCK, gp), lambda i, g: (0, i, g)),
                  pl.BlockSpec((1, ROW_BLOCK, gp), lambda i, g: (1, i, g)),
                  blk(), blk(), vec(), vec()],
        out_specs=blk(),
        out_shape=jax.ShapeDtypeStruct((t, d_ssd), BF16),
        compiler_params=_params("parallel", "arbitrary"),
    )(y2, y2, xbc, zx, d_skip_x.reshape(1, d_ssd), norm_g.reshape(1, d_ssd))


def _sgu_kernel(u_ref, v_ref, g_ref, w_ref, bt_ref, o_ref, *, groups):
    u = _gelu_tanh(u_ref[...])
    v = _gelu_tanh(v_ref[...])
    v = (v * lax.rsqrt(jnp.mean(v * v, axis=-1, keepdims=True) + EPS) * g_ref[...]).astype(BF16)
    for g in range(groups):
        sl = slice(g * GROUP_DIM, (g + 1) * GROUP_DIM)
        vm = _dot(w_ref[g], v[:, sl]) + bt_ref[:, g:g + 1]
        o_ref[:, sl] = (u[:, sl] * vm).astype(o_ref.dtype)


def sgu(uvf, norm_g, w_s, b_s):
    t = uvf.shape[0]
    groups = w_s.shape[0]
    dm = groups * GROUP_DIM
    return pl.pallas_call(
        functools.partial(_sgu_kernel, groups=groups),
        grid=(t // CHUNK,),
        in_specs=[pl.BlockSpec((CHUNK, dm), lambda c: (c, 0)),
                  pl.BlockSpec((CHUNK, dm), lambda c: (c, 1)),
                  pl.BlockSpec((1, dm), lambda c: (0, 0)),
                  pl.BlockSpec((groups, CHUNK, CHUNK), lambda c: (0, 0, 0)),
                  pl.BlockSpec((CHUNK, groups), lambda c: (0, 0))],
        out_specs=pl.BlockSpec((CHUNK, dm), lambda c: (c, 0)),
        out_shape=jax.ShapeDtypeStruct((t, dm), BF16),
        compiler_params=_params("parallel"),
    )(uvf, uvf, norm_g.reshape(1, dm), w_s.astype(BF16), b_s.T)


def _channel_dft(f, cs_tab):
    gr, gs = [], []
    for g in range(f.shape[1] // GROUP_DIM):
        p = _dot(f[:, g * GROUP_DIM:(g + 1) * GROUP_DIM], cs_tab, HI)
        gr.append(p[:, :GROUP_DIM])
        gs.append(p[:, GROUP_DIM:])
    cat = lambda v: v[0] if len(v) == 1 else jnp.concatenate(v, axis=1)
    return cat(gr), cat(gs)


def _fft_a_kernel(f_ref, m_ref, cs_ref, z_ref, *, n1):
    gr, gs = _channel_dft(f_ref[...], cs_ref[...])
    m = m_ref[...]
    z_ref[...] = _dot(m[:, :n1], gr, HI) + _dot(m[:, n1:], gs, HI)


def _fft_b_kernel(zr_ref, zi_ref, c_ref, s_ref, o_ref, *, scale):
    acc = _dot(c_ref[...], zr_ref[...], HI) - _dot(s_ref[...], zi_ref[...], HI)
    o_ref[...] = (acc * scale).astype(o_ref.dtype)


def _fft_ctx_kernel(f_ref, c_ref, s_ref, cs_ref, o_ref, *, scale):
    gr, gs = _channel_dft(f_ref[...], cs_ref[...])
    acc = _dot(c_ref[...], gr, HI) - _dot(s_ref[...], gs, HI)
    o_ref[...] = (acc * scale).astype(o_ref.dtype)


def _cos_sin(num, den):
    ang = (num % den).astype(F32) * np.float32(2.0 * np.pi / den)
    return jnp.cos(ang), jnp.sin(ang)


def fourier_mix(uvf, col0, d_fft, seq, ctx_len):
    t = uvf.shape[0]
    n2 = CHUNK
    n1 = seq // n2
    assert n1 * n2 == seq and n1 % 8 == 0 and d_fft % GROUP_DIM == 0
    ar = lambda n: jnp.arange(n, dtype=jnp.int32)
    cc, sc = _cos_sin(ar(GROUP_DIM)[:, None] * ar(GROUP_DIM)[None, :], GROUP_DIM)
    cs_tab = jnp.concatenate([cc, sc], axis=1)
    k1 = ar(n1)[None, :, None]
    tt = n2 * ar(n1)[None, None, :] + ar(n2)[:, None, None]
    mc, ms = _cos_sin(k1 * tt, seq)
    m_tab = jnp.concatenate([jnp.concatenate([mc, -ms], axis=2),
                             jnp.concatenate([ms, mc], axis=2)], axis=1)
    c2, s2 = _cos_sin(ar(n2)[:, None] * ar(n2)[None, :], n2)

    tch = _pick(d_fft, (512, 256, 128))
    f_t = uvf[:seq, col0:col0 + d_fft].reshape(n1, n2, d_fft).transpose(1, 0, 2)
    z = pl.pallas_call(
        functools.partial(_fft_a_kernel, n1=n1),
        grid=(n2, d_fft // tch),
        in_specs=[pl.BlockSpec((None, n1, tch), lambda a, j: (a, 0, j)),
                  pl.BlockSpec((None, 2 * n1, 2 * n1), lambda a, j: (a, 0, 0)),
                  pl.BlockSpec((GROUP_DIM, 2 * GROUP_DIM), lambda a, j: (0, 0))],
        out_specs=pl.BlockSpec((None, 2 * n1, tch), lambda a, j: (a, 0, j)),
        out_shape=jax.ShapeDtypeStruct((n2, 2 * n1, d_fft), F32),
        compiler_params=_params("parallel", "arbitrary"),
    )(f_t, m_tab, cs_tab)

    width = n1 * d_fft
    tw = _pick(width, (4096, 2048, 1024, 512, 256, 128))
    nb = width // tw
    z2 = z.reshape(n2, 2 * width)
    tab = lambda: pl.BlockSpec((n2, n2), lambda j: (0, 0))
    y_x = pl.pallas_call(
        functools.partial(_fft_b_kernel, scale=float(1.0 / np.sqrt(seq * GROUP_DIM))),
        grid=(nb,),
        in_specs=[pl.BlockSpec((n2, tw), lambda j: (0, j)),
                  pl.BlockSpec((n2, tw), lambda j: (0, nb + j)),
                  tab(), tab()],
        out_specs=pl.BlockSpec((n2, tw), lambda j: (0, j)),
        out_shape=jax.ShapeDtypeStruct((n2, width), BF16),
        compiler_params=_params("parallel"),
    )(z2, z2, c2, s2).reshape(seq, d_fft)

    assert ctx_len == ROW_BLOCK and col0 % d_fft == 0
    cl, sl = _cos_sin(ar(ctx_len)[:, None] * ar(ctx_len)[None, :], ctx_len)
    ctab = lambda: pl.BlockSpec((ctx_len, ctx_len), lambda j: (0, 0))
    y_c = pl.pallas_call(
        functools.partial(_fft_ctx_kernel, scale=float(1.0 / np.sqrt(ctx_len * GROUP_DIM))),
        grid=(1,),
        in_specs=[pl.BlockSpec((ctx_len, d_fft), lambda j: (seq // ctx_len, col0 // d_fft)),
                  ctab(), ctab(),
                  pl.BlockSpec((GROUP_DIM, 2 * GROUP_DIM), lambda j: (0, 0))],
        out_specs=pl.BlockSpec((ctx_len, d_fft), lambda j: (0, 0)),
        out_shape=jax.ShapeDtypeStruct((ctx_len, d_fft), BF16),
        compiler_params=_params("arbitrary"),
    )(uvf, cl, sl, cs_tab)
    return jnp.concatenate([y_x, y_c], axis=0)


def kernel(x, c, ctx, c_ctx, ada_w1, ada_w2, ada_b, norm_ffn1, f1_w1, f1_w3, f1_w2, norm_mix, w_in, b_in,
           conv_w, conv_b, a_log, dt_bias, d_skip, norm_ssd, sgu_norm, sgu_w, sgu_b,
           w_br_ssd, w_br_mlp, w_br_fft, w_out, norm_ffn2, f2_w1, f2_w3, f2_w2, norm_final):
    batch, seq, d = x.shape
    ctx_len = ctx.shape[1]
    depth = ada_w1.shape[0]
    assert batch == 1 and seq % ROW_BLOCK == 0
    heads = a_log.shape[-1]
    d_ssd = norm_ssd.shape[-1]
    c_xbc = conv_w.shape[-1]
    d_state = (c_xbc - d_ssd) // (2 * SSD_GROUPS)
    d_mlp = sgu_norm.shape[-1]
    d_fft = w_br_fft.shape[1]
    r_heads = heads // SSD_GROUPS
    gp = d_ssd // SSD_GROUPS
    assert heads * SSD_HEAD_DIM == d_ssd and 2 * heads <= LANES and d_mlp == d_fft
    off_xbc = d_ssd
    off_dt = off_xbc + c_xbc
    off_u = off_dt + 2 * heads
    off_gate = off_u + 2 * d_mlp + d_fft

    sel = np.zeros((2 * SSD_GROUPS, LANES, LANES), np.float32)
    for dg in range(2 * SSD_GROUPS):
        for r in range(r_heads):
            sel[dg, dg * r_heads + r, r] = 1.0
    expand = np.zeros((LANES, gp), np.float32)
    for r in range(r_heads):
        expand[r, r * SSD_HEAD_DIM:(r + 1) * SSD_HEAD_DIM] = 1.0
    sel, expand = jnp.asarray(sel), jnp.asarray(expand)

    cv = jnp.zeros((8, d), F32).at[0].set(c[0]).at[1].set(c_ctx)
    mods = ada_mods(cv, ada_w1, ada_w2, ada_b)[:, 0:2, :].reshape(depth, 2, N_MOD, d)
    xc = jnp.concatenate([x[0], ctx[0]], axis=0)

    for i in range(depth):
        sh1, sc1, g1, shm, scm, gm, sh2, sc2, g2 = (mods[i, :, k, :] for k in range(N_MOD))
        bf = lambda w: w.astype(BF16)

        h = norm_mod(xc, norm_ffn1[i], sh1, sc1, seq)
        xc = down_residual(swiglu_up(h, bf(f1_w1[i]), bf(f1_w3[i])), bf(f1_w2[i]), xc, g1, 0.5, seq)

        h = norm_mod(xc, norm_mix[i], shm, scm, seq)
        wi, bi = w_in[i], b_in[i]
        zx = proj(h, bf(wi[:, :off_dt]), bi[:off_dt], F32)
        wdt = jnp.pad(wi[:, off_dt:off_u], ((0, 0), (0, LANES - 2 * heads)))
        dt_raw = proj(h, bf(wdt), jnp.pad(bi[off_dt:off_u], (0, LANES - 2 * heads)), F32)
        uvf = proj(h, bf(wi[:, off_u:off_gate]), bi[off_u:off_gate], F32)
        gates = proj(h, bf(wi[:, off_gate:]), bi[off_gate:], BF16, sigmoid=True)

        xbc = conv_silu(zx, off_xbc, conv_w[i], conv_b[i], seq, ctx_len)
        a_neg = -jnp.exp(a_log[i].astype(F32)).reshape(-1)
        dt, cs, tot, csrow = dt_prep(dt_raw, dt_bias[i].reshape(-1), a_neg, sel)
        y2 = ssd_scan(xbc, dt, cs, tot, csrow, expand, seq, d_ssd, d_state)
        y_ssd = ssd_out(y2, xbc, zx, jnp.repeat(d_skip[i], SSD_HEAD_DIM), norm_ssd[i])
        y_mlp = sgu(uvf, sgu_norm[i], sgu_w[i], sgu_b[i])
        y_fft = fourier_mix(uvf, 2 * d_mlp, d_fft, seq, ctx_len)

        merged = merge(y_ssd, y_mlp, y_fft, bf(w_br_ssd[i]), bf(w_br_mlp[i]), bf(w_br_fft[i]), gates)
        xc = down_residual(merged, bf(w_out[i]), xc, gm, 1.0, seq)

        h = norm_mod(xc, norm_ffn2[i], sh2, sc2, seq)
        xc = down_residual(swiglu_up(h, bf(f2_w1[i]), bf(f2_w3[i])), bf(f2_w2[i]), xc, g2, 0.5, seq)

    return final_norm(xc, norm_final, seq).reshape(1, seq, d)
```

```python
import functools

import numpy as np
import jax
import jax.numpy as jnp
from jax import lax
from jax.experimental import pallas as pl
from jax.experimental.pallas import tpu as pltpu

F32 = jnp.float32
BF16 = jnp.bfloat16
HI = lax.Precision.HIGHEST

EPS = 1e-6
GRID_W = 64
SSD_GROUPS = 4
SSD_HEAD_DIM = 64
CHUNK = 128
D_CONV = 5
GROUP_DIM = 128
N_MOD = 9
LANES = 128
ROW_BLOCK = 256
VMEM_LIMIT = 56 * 1024 * 1024


def _pick(n, cands):
    for c in cands:
        if n % c == 0:
            return c
    raise ValueError(f"no tile for {n} in {cands}")


def _params(*sem):
    return pltpu.CompilerParams(dimension_semantics=sem, vmem_limit_bytes=VMEM_LIMIT)


def _silu(v):
    return v * jax.nn.sigmoid(v)


def _gelu_tanh(v):
    return v * (0.5 * (1.0 + jnp.tanh(np.sqrt(2.0 / np.pi).astype(np.float32) * (v + 0.044715 * (v * v * v)))))


def _dot(a, b, precision=None):
    return jnp.dot(a, b, preferred_element_type=F32, precision=precision)


def _ada_kernel(cv_ref, w1_ref, w2_ref, b_ref, o_ref):
    t = _dot(_silu(cv_ref[...]), w1_ref[0], HI)
    o_ref[0] = _dot(t, w2_ref[0], HI) + b_ref[0]


def ada_mods(cv, w1, w2, b):
    nl, d, r = w1.shape
    n = w2.shape[-1]
    tn = _pick(n, (4608, 4096, 2304, 2048, 1024, 512, 256, 128))
    return pl.pallas_call(
        _ada_kernel,
        grid=(nl, n // tn),
        in_specs=[pl.BlockSpec((8, d), lambda l, j: (0, 0)),
                  pl.BlockSpec((1, d, r), lambda l, j: (l, 0, 0)),
                  pl.BlockSpec((1, r, tn), lambda l, j: (l, 0, j)),
                  pl.BlockSpec((1, 1, tn), lambda l, j: (l, 0, j))],
        out_specs=pl.BlockSpec((1, 8, tn), lambda l, j: (l, 0, j)),
        out_shape=jax.ShapeDtypeStruct((nl, 8, n), F32),
        compiler_params=_params("arbitrary", "arbitrary"),
        name="ada_mods",
    )(cv, w1, w2, b.reshape(nl, 1, n))


def _norm_mod_kernel(x_ref, g_ref, sh_ref, sc_ref, o_ref):
    x = x_ref[...]
    y = x * lax.rsqrt(jnp.mean(x * x, axis=-1, keepdims=True) + EPS) * g_ref[...]
    o_ref[...] = (y * (1.0 + sc_ref[0]) + sh_ref[0]).astype(o_ref.dtype)


def norm_mod(xc, g, shift, scale, seq):
    t, d = xc.shape
    nx = seq // ROW_BLOCK
    sel = lambda i: (jnp.where(i >= nx, 1, 0), 0, 0)
    return pl.pallas_call(
        _norm_mod_kernel,
        grid=(t // ROW_BLOCK,),
        in_specs=[pl.BlockSpec((ROW_BLOCK, d), lambda i: (i, 0)),
                  pl.BlockSpec((1, d), lambda i: (0, 0)),
                  pl.BlockSpec((1, 1, d), sel),
                  pl.BlockSpec((1, 1, d), sel)],
        out_specs=pl.BlockSpec((ROW_BLOCK, d), lambda i: (i, 0)),
        out_shape=jax.ShapeDtypeStruct((t, d), BF16),
        compiler_params=_params("parallel"),
        name="norm_mod",
    )(xc, g.reshape(1, d), shift.reshape(2, 1, d), scale.reshape(2, 1, d))


def _final_norm_kernel(x_ref, g_ref, o_ref):
    x = x_ref[...]
    o_ref[...] = x * lax.rsqrt(jnp.mean(x * x, axis=-1, keepdims=True) + EPS) * g_ref[...]


def final_norm(xc, g, seq):
    d = xc.shape[1]
    return pl.pallas_call(
        _final_norm_kernel,
        grid=(seq // ROW_BLOCK,),
        in_specs=[pl.BlockSpec((ROW_BLOCK, d), lambda i: (i, 0)),
                  pl.BlockSpec((1, d), lambda i: (0, 0))],
        out_specs=pl.BlockSpec((ROW_BLOCK, d), lambda i: (i, 0)),
        out_shape=jax.ShapeDtypeStruct((seq, d), F32),
        compiler_params=_params("parallel"),
        name="final_norm",
    )(xc, g.reshape(1, d))


def _mm_tiles(t, n):
    tm = _pick(t, (1280, 1024, 768, 640, 512, 384, 256))
    tn = _pick(n, (512, 384, 256, 128))
    return tm, tn


def _up_kernel(h_ref, w1_ref, w3_ref, o_ref):
    h = h_ref[...]
    a = _dot(h, w1_ref[...])
    o_ref[...] = (_silu(a) * _dot(h, w3_ref[...])).astype(o_ref.dtype)


def swiglu_up(h, w1, w3):
    t, d = h.shape
    n = w1.shape[1]
    tm, tn = _mm_tiles(t, n)
    return pl.pallas_call(
        _up_kernel,
        grid=(t // tm, n // tn),
        in_specs=[pl.BlockSpec((tm, d), lambda i, j: (i, 0)),
                  pl.BlockSpec((d, tn), lambda i, j: (0, j)),
                  pl.BlockSpec((d, tn), lambda i, j: (0, j))],
        out_specs=pl.BlockSpec((tm, tn), lambda i, j: (i, j)),
        out_shape=jax.ShapeDtypeStruct((t, n), BF16),
        compiler_params=_params("parallel", "arbitrary"),
        name="swiglu_up",
    )(h, w1, w3)


def _down_kernel(a_ref, w_ref, res_ref, gate_ref, o_ref, *, coef, tm, seq):
    acc = _dot(a_ref[...], w_ref[...])
    rows = pl.program_id(0) * tm + lax.broadcasted_iota(jnp.int32, (tm, 1), 0)
    gate = jnp.where(rows >= seq, gate_ref[1:2, :], gate_ref[0:1, :])
    o_ref[...] = res_ref[...] + (coef * gate) * acc


def down_residual(a, w, res, gate, coef, seq):
    t, k = a.shape
    n = w.shape[1]
    tm, tn = _mm_tiles(t, n)
    return pl.pallas_call(
        functools.partial(_down_kernel, coef=coef, tm=tm, seq=seq),
        grid=(t // tm, n // tn),
        in_specs=[pl.BlockSpec((tm, k), lambda i, j: (i, 0)),
                  pl.BlockSpec((k, tn), lambda i, j: (0, j)),
                  pl.BlockSpec((tm, tn), lambda i, j: (i, j)),
                  pl.BlockSpec((2, tn), lambda i, j: (0, j))],
        out_specs=pl.BlockSpec((tm, tn), lambda i, j: (i, j)),
        out_shape=jax.ShapeDtypeStruct((t, n), F32),
        input_output_aliases={2: 0},
        compiler_params=_params("parallel", "arbitrary"),
        name="down_residual",
    )(a, w, res, gate)


def _proj_kernel(h_ref, w_ref, b_ref, o_ref, *, sigmoid):
    acc = _dot(h_ref[...], w_ref[...]) + b_ref[...]
    if sigmoid:
        acc = jax.nn.sigmoid(acc)
    o_ref[...] = acc.astype(o_ref.dtype)


def proj(h, w, b, out_dtype, sigmoid=False, name="proj"):
    t, d = h.shape
    n = w.shape[1]
    tm, tn = _mm_tiles(t, n)
    return pl.pallas_call(
        functools.partial(_proj_kernel, sigmoid=sigmoid),
        grid=(t // tm, n // tn),
        in_specs=[pl.BlockSpec((tm, d), lambda i, j: (i, 0)),
                  pl.BlockSpec((d, tn), lambda i, j: (0, j)),
                  pl.BlockSpec((1, tn), lambda i, j: (0, j))],
        out_specs=pl.BlockSpec((tm, tn), lambda i, j: (i, j)),
        out_shape=jax.ShapeDtypeStruct((t, n), out_dtype),
        compiler_params=_params("parallel", "arbitrary"),
        name=name,
    )(h, w, b.reshape(1, n))


def _merge_kernel(ys_ref, ym_ref, yf_ref, ws_ref, wm_ref, wf_ref, gs_ref, gm_ref, gf_ref, o_ref):
    acc = gs_ref[...].astype(F32) * _dot(ys_ref[...], ws_ref[...])
    acc = acc + gm_ref[...].astype(F32) * _dot(ym_ref[...], wm_ref[...])
    acc = acc + gf_ref[...].astype(F32) * _dot(yf_ref[...], wf_ref[...])
    o_ref[...] = acc.astype(o_ref.dtype)


def merge(y_ssd, y_mlp, y_fft, w_ssd, w_mlp, w_fft, gates):
    t = y_ssd.shape[0]
    n = w_ssd.shape[1]
    tm, tn = _mm_tiles(t, n)
    nb = n // tn
    row = lambda k: pl.BlockSpec((tm, k), lambda i, j: (i, 0))
    col = lambda k: pl.BlockSpec((k, tn), lambda i, j: (0, j))
    gate = lambda b: pl.BlockSpec((tm, tn), lambda i, j: (i, b * nb + j))
    return pl.pallas_call(
        _merge_kernel,
        grid=(t // tm, nb),
        in_specs=[row(y_ssd.shape[1]), row(y_mlp.shape[1]), row(y_fft.shape[1]),
                  col(w_ssd.shape[0]), col(w_mlp.shape[0]), col(w_fft.shape[0]),
                  gate(0), gate(1), gate(2)],
        out_specs=pl.BlockSpec((tm, tn), lambda i, j: (i, j)),
        out_shape=jax.ShapeDtypeStruct((t, n), BF16),
        compiler_params=_params("parallel", "arbitrary"),
        name="merge",
    )(y_ssd, y_mlp, y_fft, w_ssd, w_mlp, w_fft, gates, gates, gates)


def _conv_kernel(u_ref, w_ref, b_ref, o_ref, *, nx, ctx_len):
    u = u_ref[...].astype(F32)
    rb = u.shape[0]
    width = jnp.where(pl.program_id(0) >= nx, ctx_len, GRID_W)
    pos = lax.broadcasted_iota(jnp.int32, (rb, 1), 0) & (width - 1)
    y = jnp.broadcast_to(b_ref[...], u.shape)
    for k in range(D_CONV):
        off = k - D_CONV // 2
        shifted = u if off == 0 else pltpu.roll(u, (-off) % rb, 0)
        valid = (pos + off >= 0) & (pos + off < width)
        y = y + jnp.where(valid, shifted, 0.0) * w_ref[k:k + 1, :]
    o_ref[...] = _silu(y).astype(o_ref.dtype)


def conv_silu(zx, col0, conv_w, conv_b, seq, ctx_len):
    t = zx.shape[0]
    c = conv_w.shape[1]
    tc = next(v for v in (1024, 512, 256, 128) if c % v == 0 and col0 % v == 0)
    assert ctx_len == ROW_BLOCK and ROW_BLOCK % GRID_W == 0
    assert GRID_W & (GRID_W - 1) == 0 and ctx_len & (ctx_len - 1) == 0
    nx = seq // ROW_BLOCK
    return pl.pallas_call(
        functools.partial(_conv_kernel, nx=nx, ctx_len=ctx_len),
        grid=(t // ROW_BLOCK, c // tc),
        in_specs=[pl.BlockSpec((ROW_BLOCK, tc), lambda i, j: (i, col0 // tc + j)),
                  pl.BlockSpec((D_CONV, tc), lambda i, j: (0, j)),
                  pl.BlockSpec((1, tc), lambda i, j: (0, j))],
        out_specs=pl.BlockSpec((ROW_BLOCK, tc), lambda i, j: (i, j)),
        out_shape=jax.ShapeDtypeStruct((t, c), BF16),
        compiler_params=_params("parallel", "arbitrary"),
        name="conv_silu",
    )(zx, conv_w, conv_b.reshape(1, c))


def _expand_heads(arr, h0, r_heads):
    rows = arr.shape[0]
    lo = lax.broadcasted_iota(jnp.int32, (rows, LANES), 1) < SSD_HEAD_DIM
    parts = []
    for k in range(r_heads // 2):
        a = jnp.broadcast_to(arr[:, h0 + 2 * k:h0 + 2 * k + 1], (rows, LANES))
        b = jnp.broadcast_to(arr[:, h0 + 2 * k + 1:h0 + 2 * k + 2], (rows, LANES))
        parts.append(jnp.where(lo, a, b))
    return parts[0] if len(parts) == 1 else jnp.concatenate(parts, axis=1)


def _ssd_kernel(*refs, direction, heads, d_ssd, d_state, final):
    if final:
        x_ref, raw_ref, bias_ref, a_ref, yf_ref, z_ref, dsk_ref, g_ref, o_ref, state_ref = refs
    else:
        x_ref, raw_ref, bias_ref, a_ref, o_ref, state_ref = refs
    gp = d_ssd // SSD_GROUPS
    r_heads = gp // SSD_HEAD_DIM
    gn = SSD_GROUPS * d_state

    @pl.when(pl.program_id(0) == 0)
    def _():
        state_ref[...] = jnp.zeros_like(state_ref)

    v = raw_ref[...] + bias_ref[...]
    dt = jnp.maximum(v, 0.0) + jnp.log1p(jnp.exp(-jnp.abs(v)))
    qi = lax.broadcasted_iota(jnp.int32, (CHUNK, CHUNK), 0)
    si = lax.broadcasted_iota(jnp.int32, (CHUNK, CHUNK), 1)
    causal = (si <= qi) if direction == 0 else (si >= qi)
    cs = _dot(causal.astype(F32), dt * a_ref[...], HI)
    tot = cs[CHUNK - 1:CHUNK, :] if direction == 0 else cs[0:1, :]
    ecs = jnp.exp(cs)
    wst = jnp.exp(tot - cs) * dt
    etot = jnp.exp(tot)
    cs_t = cs.T
    dt_t = dt.T
    lo = lax.broadcasted_iota(jnp.int32, (CHUNK, LANES), 1) < SSD_HEAD_DIM

    for g in range(SSD_GROUPS):
        h0 = direction * heads + g * r_heads
        gs = slice(g * gp, (g + 1) * gp)
        xs = x_ref[:, gs].astype(F32)
        bm = x_ref[:, d_ssd + g * d_state:d_ssd + (g + 1) * d_state]
        cm = x_ref[:, d_ssd + gn + g * d_state:d_ssd + gn + (g + 1) * d_state]
        cb = lax.dot_general(cm, bm, (((1,), (1,)), ((), ())), preferred_element_type=F32)
        state = state_ref[:, gs]
        y = _dot(cm, state.astype(BF16)) * _expand_heads(ecs, h0, r_heads)
        parts = []
        for k in range(r_heads // 2):
            xd = xs[:, k * LANES:(k + 1) * LANES]
            acc = None
            for half in range(2):
                h = h0 + 2 * k + half
                diff = cs[:, h:h + 1] - cs_t[h:h + 1, :]
                m = (cb * (jnp.exp(jnp.where(causal, diff, -1e30)) * dt_t[h:h + 1, :])).astype(BF16)
                xh = (jnp.where(lo, xd, 0.0) if half == 0 else jnp.where(lo, 0.0, xd)).astype(BF16)
                p = _dot(m, xh)
                acc = p if acc is None else acc + p
            parts.append(acc)
        y = y + (parts[0] if len(parts) == 1 else jnp.concatenate(parts, axis=1))

        xw = (xs * _expand_heads(wst, h0, r_heads)).astype(BF16)
        bt = bm.astype(F32).T.astype(BF16)
        state_ref[:, gs] = state * _expand_heads(etot, h0, r_heads) + _dot(bt, xw)

        if final:
            y = (yf_ref[:, gs] + y + dsk_ref[:, gs] * xs) * _silu(z_ref[:, gs].astype(F32))
            y = y * lax.rsqrt(jnp.mean(y * y, axis=-1, keepdims=True) + EPS) * g_ref[:, gs]
        o_ref[:, gs] = y.astype(o_ref.dtype)


def ssd_mixer(xbc, zx, dt_raw, dt_bias, a_neg, d_skip_x, norm_g, seq, d_ssd, d_state):
    t, c_xbc = xbc.shape
    nc = t // CHUNK
    nxc = seq // CHUNK
    ncc = nc - nxc
    heads = dt_bias.shape[0] // 2
    gp = d_ssd // SSD_GROUPS
    assert d_state == LANES and (gp // SSD_HEAD_DIM) % 2 == 0 and gp % LANES == 0 and 2 * heads <= LANES
    pad = lambda vec: jnp.pad(vec.reshape(1, -1), ((0, 0), (0, LANES - vec.shape[0])))
    fwd = lambda j: jnp.where(j < ncc, nxc + j, j - ncc)
    bwd = lambda j: nc - 1 - j
    vec = lambda n: pl.BlockSpec((1, n), lambda j: (0, 0))

    def call(direction, order, final, extra_in, extra_specs, out_dtype):
        return pl.pallas_call(
            functools.partial(_ssd_kernel, direction=direction, heads=heads, d_ssd=d_ssd, d_state=d_state,
                              final=final),
            grid=(nc,),
            in_specs=[pl.BlockSpec((CHUNK, c_xbc), lambda j: (order(j), 0)),
                      pl.BlockSpec((CHUNK, LANES), lambda j: (order(j), 0)),
                      vec(LANES), vec(LANES)] + extra_specs,
            out_specs=pl.BlockSpec((CHUNK, d_ssd), lambda j: (order(j), 0)),
            out_shape=jax.ShapeDtypeStruct((t, d_ssd), out_dtype),
            scratch_shapes=[pltpu.VMEM((d_state, d_ssd), F32)],
            compiler_params=_params("arbitrary"),
            name="ssd_bwd_out" if final else "ssd_fwd",
        )(xbc, dt_raw, pad(dt_bias), pad(a_neg), *extra_in)

    y_f = call(0, fwd, False, [], [], F32)
    blk = pl.BlockSpec((CHUNK, d_ssd), lambda j: (bwd(j), 0))
    return call(1, bwd, True,
                [y_f, zx, d_skip_x.reshape(1, d_ssd), norm_g.reshape(1, d_ssd)],
                [blk, blk, vec(d_ssd), vec(d_ssd)], BF16)


def _sgu_kernel(u_ref, v_ref, g_ref, w_ref, bt_ref, o_ref, *, groups):
    u = _gelu_tanh(u_ref[...].astype(F32))
    v = _gelu_tanh(v_ref[...].astype(F32))
    v = (v * lax.rsqrt(jnp.mean(v * v, axis=-1, keepdims=True) + EPS) * g_ref[...]).astype(BF16)
    for g in range(groups):
        sl = slice(g * GROUP_DIM, (g + 1) * GROUP_DIM)
        vm = _dot(w_ref[g], v[:, sl]) + bt_ref[:, g:g + 1]
        o_ref[:, sl] = (u[:, sl] * vm).astype(o_ref.dtype)


def sgu(uvf, norm_g, w_s, b_s):
    t = uvf.shape[0]
    groups = w_s.shape[0]
    dm = groups * GROUP_DIM
    return pl.pallas_call(
        functools.partial(_sgu_kernel, groups=groups),
        grid=(t // CHUNK,),
        in_specs=[pl.BlockSpec((CHUNK, dm), lambda c: (c, 0)),
                  pl.BlockSpec((CHUNK, dm), lambda c: (c, 1)),
                  pl.BlockSpec((1, dm), lambda c: (0, 0)),
                  pl.BlockSpec((groups, CHUNK, CHUNK), lambda c: (0, 0, 0)),
                  pl.BlockSpec((CHUNK, groups), lambda c: (0, 0))],
        out_specs=pl.BlockSpec((CHUNK, dm), lambda c: (c, 0)),
        out_shape=jax.ShapeDtypeStruct((t, dm), BF16),
        compiler_params=_params("parallel"),
        name="sgu",
    )(uvf, uvf, norm_g.reshape(1, dm), w_s.astype(BF16), b_s.T)


def _channel_dft(f, cs_tab):
    gr, gs = [], []
    for g in range(f.shape[1] // GROUP_DIM):
        p = _dot(f[:, g * GROUP_DIM:(g + 1) * GROUP_DIM], cs_tab)
        gr.append(p[:, :GROUP_DIM])
        gs.append(p[:, GROUP_DIM:])
    cat = lambda v: (v[0] if len(v) == 1 else jnp.concatenate(v, axis=1)).astype(BF16)
    return cat(gr), cat(gs)


def _fft_a_kernel(f_ref, m_ref, cs_ref, z_ref, *, n1):
    gr, gs = _channel_dft(f_ref[...], cs_ref[...])
    m = m_ref[...]
    z_ref[...] = (_dot(m[:, :n1], gr) + _dot(m[:, n1:], gs)).astype(z_ref.dtype)


def _fft_b_kernel(zr_ref, zi_ref, c_ref, s_ref, o_ref, *, scale):
    acc = _dot(c_ref[...], zr_ref[...]) - _dot(s_ref[...], zi_ref[...])
    o_ref[...] = (acc * scale).astype(o_ref.dtype)


def _fft_ctx_kernel(f_ref, c_ref, s_ref, cs_ref, o_ref, *, scale):
    gr, gs = _channel_dft(f_ref[...], cs_ref[...])
    acc = _dot(c_ref[...], gr) - _dot(s_ref[...], gs)
    o_ref[...] = (acc * scale).astype(o_ref.dtype)


def _cos_sin(n):
    ang = 2.0 * np.pi * (np.outer(np.arange(n), np.arange(n)) % n) / n
    return np.cos(ang), np.sin(ang)


def fourier_mix(uvf, col0, d_fft, seq, ctx_len):
    n2 = CHUNK
    n1 = seq // n2
    assert n1 * n2 == seq and n1 % 16 == 0 and d_fft % GROUP_DIM == 0
    cc, sc = _cos_sin(GROUP_DIM)
    cs_tab = jnp.asarray(np.concatenate([cc, sc], axis=1), BF16)
    c1, s1 = _cos_sin(n1)
    ang = 2.0 * np.pi * np.outer(np.arange(n2), np.arange(n1)) / seq
    ct, st = (jnp.asarray(v[:, :, None], F32) for v in (np.cos(ang), np.sin(ang)))
    c1, s1 = jnp.asarray(c1[None], F32), jnp.asarray(s1[None], F32)
    mc = c1 * ct - s1 * st
    ms = s1 * ct + c1 * st
    m_tab = jnp.concatenate([jnp.concatenate([mc, -ms], axis=2),
                             jnp.concatenate([ms, mc], axis=2)], axis=1).astype(BF16)
    c2, s2 = (jnp.asarray(v, BF16) for v in _cos_sin(n2))

    f_t = uvf[:seq, col0:col0 + d_fft].reshape(n1, n2, d_fft).transpose(1, 0, 2)
    z = pl.pallas_call(
        functools.partial(_fft_a_kernel, n1=n1),
        grid=(n2,),
        in_specs=[pl.BlockSpec((None, n1, d_fft), lambda a: (a, 0, 0)),
                  pl.BlockSpec((None, 2 * n1, 2 * n1), lambda a: (a, 0, 0)),
                  pl.BlockSpec((GROUP_DIM, 2 * GROUP_DIM), lambda a: (0, 0))],
        out_specs=pl.BlockSpec((None, 2 * n1, d_fft), lambda a: (a, 0, 0)),
        out_shape=jax.ShapeDtypeStruct((n2, 2 * n1, d_fft), BF16),
        compiler_params=_params("parallel"),
        name="fft_stage_a",
    )(f_t, m_tab, cs_tab)

    width = n1 * d_fft
    tw = _pick(width, (8192, 4096, 2048, 1024, 512, 256, 128))
    nb = width // tw
    z2 = z.reshape(n2, 2 * width)
    tab = lambda: pl.BlockSpec((n2, n2), lambda j: (0, 0))
    y_x = pl.pallas_call(
        functools.partial(_fft_b_kernel, scale=float(1.0 / np.sqrt(seq * GROUP_DIM))),
        grid=(nb,),
        in_specs=[pl.BlockSpec((n2, tw), lambda j: (0, j)),
                  pl.BlockSpec((n2, tw), lambda j: (0, nb + j)),
                  tab(), tab()],
        out_specs=pl.BlockSpec((n2, tw), lambda j: (0, j)),
        out_shape=jax.ShapeDtypeStruct((n2, width), BF16),
        compiler_params=_params("parallel"),
        name="fft_stage_b",
    )(z2, z2, c2, s2).reshape(seq, d_fft)

    assert ctx_len == ROW_BLOCK and col0 % d_fft == 0
    cl, sl = (jnp.asarray(v, BF16) for v in _cos_sin(ctx_len))
    ctab = lambda: pl.BlockSpec((ctx_len, ctx_len), lambda j: (0, 0))
    y_c = pl.pallas_call(
        functools.partial(_fft_ctx_kernel, scale=float(1.0 / np.sqrt(ctx_len * GROUP_DIM))),
        grid=(1,),
        in_specs=[pl.BlockSpec((ctx_len, d_fft), lambda j: (seq // ctx_len, col0 // d_fft)),
                  ctab(), ctab(),
                  pl.BlockSpec((GROUP_DIM, 2 * GROUP_DIM), lambda j: (0, 0))],
        out_specs=pl.BlockSpec((ctx_len, d_fft), lambda j: (0, 0)),
        out_shape=jax.ShapeDtypeStruct((ctx_len, d_fft), BF16),
        compiler_params=_params("arbitrary"),
        name="fft_ctx",
    )(uvf, cl, sl, cs_tab)
    return jnp.concatenate([y_x, y_c], axis=0)


def kernel(x, c, ctx, c_ctx, ada_w1, ada_w2, ada_b, norm_ffn1, f1_w1, f1_w3, f1_w2, norm_mix, w_in, b_in,
           conv_w, conv_b, a_log, dt_bias, d_skip, norm_ssd, sgu_norm, sgu_w, sgu_b,
           w_br_ssd, w_br_mlp, w_br_fft, w_out, norm_ffn2, f2_w1, f2_w3, f2_w2, norm_final):
    batch, seq, d = x.shape
    ctx_len = ctx.shape[1]
    depth = ada_w1.shape[0]
    assert batch == 1 and seq % ROW_BLOCK == 0
    heads = a_log.shape[-1]
    d_ssd = norm_ssd.shape[-1]
    c_xbc = conv_w.shape[-1]
    d_state = (c_xbc - d_ssd) // (2 * SSD_GROUPS)
    d_mlp = sgu_norm.shape[-1]
    d_fft = w_br_fft.shape[1]
    assert heads * SSD_HEAD_DIM == d_ssd and d_mlp == d_fft
    off_xbc = d_ssd
    off_dt = off_xbc + c_xbc
    off_u = off_dt + 2 * heads
    off_gate = off_u + 2 * d_mlp + d_fft

    cv = jnp.zeros((8, d), F32).at[0].set(c[0]).at[1].set(c_ctx)
    mods = ada_mods(cv, ada_w1, ada_w2, ada_b)[:, 0:2, :].reshape(depth, 2, N_MOD, d)
    xc = jnp.concatenate([x[0], ctx[0]], axis=0)

    for i in range(depth):
        sh1, sc1, g1, shm, scm, gm, sh2, sc2, g2 = (mods[i, :, k, :] for k in range(N_MOD))
        bf = lambda w: w.astype(BF16)

        h = norm_mod(xc, norm_ffn1[i], sh1, sc1, seq)
        xc = down_residual(swiglu_up(h, bf(f1_w1[i]), bf(f1_w3[i])), bf(f1_w2[i]), xc, g1, 0.5, seq)

        h = norm_mod(xc, norm_mix[i], shm, scm, seq)
        wi, bi = w_in[i], b_in[i]
        zx = proj(h, bf(wi[:, :off_dt]), bi[:off_dt], BF16, name="proj_zx")
        wdt = jnp.pad(wi[:, off_dt:off_u], ((0, 0), (0, LANES - 2 * heads)))
        dt_raw = proj(h, bf(wdt), jnp.pad(bi[off_dt:off_u], (0, LANES - 2 * heads)), F32, name="proj_dt")
        uvf = proj(h, bf(wi[:, off_u:off_gate]), bi[off_u:off_gate], BF16, name="proj_uvf")
        gates = proj(h, bf(wi[:, off_gate:]), bi[off_gate:], BF16, sigmoid=True, name="proj_gates")

        xbc = conv_silu(zx, off_xbc, conv_w[i], conv_b[i], seq, ctx_len)
        a_neg = -jnp.exp(a_log[i].astype(F32)).reshape(-1)
        y_ssd = ssd_mixer(xbc, zx, dt_raw, dt_bias[i].reshape(-1), a_neg,
                          jnp.repeat(d_skip[i], SSD_HEAD_DIM), norm_ssd[i], seq, d_ssd, d_state)
        y_mlp = sgu(uvf, sgu_norm[i], sgu_w[i], sgu_b[i])
        y_fft = fourier_mix(uvf, 2 * d_mlp, d_fft, seq, ctx_len)

        merged = merge(y_ssd, y_mlp, y_fft, bf(w_br_ssd[i]), bf(w_br_mlp[i]), bf(w_br_fft[i]), gates)
        xc = down_residual(merged, bf(w_out[i]), xc, gm, 1.0, seq)

        h = norm_mod(xc, norm_ffn2[i], sh2, sc2, seq)
        xc = down_residual(swiglu_up(h, bf(f2_w1[i]), bf(f2_w3[i])), bf(f2_w2[i]), xc, g2, 0.5, seq)

    return final_norm(xc, norm_final, seq).reshape(1, seq, d)
```

```python
import functools

import numpy as np
import jax
import jax.numpy as jnp
from jax import lax
from jax.experimental import pallas as pl
from jax.experimental.pallas import tpu as pltpu

F32 = jnp.float32
BF16 = jnp.bfloat16
HI = lax.Precision.HIGHEST

EPS = 1e-6
GRID_W = 64
SSD_GROUPS = 4
SSD_HEAD_DIM = 64
CHUNK = 128
D_CONV = 5
GROUP_DIM = 128
N_MOD = 9
LANES = 128
ROW_BLOCK = 256
VMEM_LIMIT = 56 * 1024 * 1024


def _pick(n, cands):
    for c in cands:
        if n % c == 0:
            return c
    raise ValueError(f"no tile for {n} in {cands}")


def _params(*sem):
    return pltpu.CompilerParams(dimension_semantics=sem, vmem_limit_bytes=VMEM_LIMIT)


def _silu(v):
    return v * jax.nn.sigmoid(v)


def _gelu_tanh(v):
    return v * (0.5 * (1.0 + jnp.tanh(np.sqrt(2.0 / np.pi).astype(np.float32) * (v + 0.044715 * (v * v * v)))))


def _dot(a, b, precision=None):
    return jnp.dot(a, b, preferred_element_type=F32, precision=precision)


def _ada_kernel(cv_ref, w1_ref, w2_ref, b_ref, o_ref):
    t = _dot(_silu(cv_ref[...]), w1_ref[0], HI)
    o_ref[0] = _dot(t, w2_ref[0], HI) + b_ref[0]


def ada_mods(cv, w1, w2, b):
    nl, d, r = w1.shape
    n = w2.shape[-1]
    tn = _pick(n, (4608, 4096, 2304, 2048, 1024, 512, 256, 128))
    return pl.pallas_call(
        _ada_kernel,
        grid=(nl, n // tn),
        in_specs=[pl.BlockSpec((8, d), lambda l, j: (0, 0)),
                  pl.BlockSpec((1, d, r), lambda l, j: (l, 0, 0)),
                  pl.BlockSpec((1, r, tn), lambda l, j: (l, 0, j)),
                  pl.BlockSpec((1, 1, tn), lambda l, j: (l, 0, j))],
        out_specs=pl.BlockSpec((1, 8, tn), lambda l, j: (l, 0, j)),
        out_shape=jax.ShapeDtypeStruct((nl, 8, n), F32),
        compiler_params=_params("arbitrary", "arbitrary"),
        name="ada_mods",
    )(cv, w1, w2, b.reshape(nl, 1, n))


def _norm_mod_kernel(*refs, nx, split):
    if split:
        x_ref, c_ref, g_ref, sh_ref, sc_ref, o_ref = refs
    else:
        x_ref, g_ref, sh_ref, sc_ref, o_ref = refs

    def emit(src_ref):
        x = src_ref[...]
        y = x * lax.rsqrt(jnp.mean(x * x, axis=-1, keepdims=True) + EPS) * g_ref[...]
        o_ref[...] = (y * (1.0 + sc_ref[0]) + sh_ref[0]).astype(o_ref.dtype)

    if split:
        pl.when(pl.program_id(0) < nx)(lambda: emit(x_ref))
        pl.when(pl.program_id(0) >= nx)(lambda: emit(c_ref))
    else:
        emit(x_ref)


def norm_mod(xs, g, shift, scale, seq):
    split = isinstance(xs, tuple)
    d = g.shape[0]
    nx = seq // ROW_BLOCK
    sel = lambda i: (jnp.where(i >= nx, 1, 0), 0, 0)
    if split:
        t = seq + xs[1].shape[0]
        src_specs = [pl.BlockSpec((ROW_BLOCK, d), lambda i: (jnp.minimum(i, nx - 1), 0)),
                     pl.BlockSpec((ROW_BLOCK, d), lambda i: (jnp.maximum(i - nx, 0), 0))]
    else:
        xs = (xs,)
        t = xs[0].shape[0]
        src_specs = [pl.BlockSpec((ROW_BLOCK, d), lambda i: (i, 0))]
    return pl.pallas_call(
        functools.partial(_norm_mod_kernel, nx=nx, split=split),
        grid=(t // ROW_BLOCK,),
        in_specs=src_specs + [pl.BlockSpec((1, d), lambda i: (0, 0)),
                              pl.BlockSpec((1, 1, d), sel),
                              pl.BlockSpec((1, 1, d), sel)],
        out_specs=pl.BlockSpec((ROW_BLOCK, d), lambda i: (i, 0)),
        out_shape=jax.ShapeDtypeStruct((t, d), BF16),
        compiler_params=_params("parallel"),
        name="norm_mod",
    )(*xs, g.reshape(1, d), shift.reshape(2, 1, d), scale.reshape(2, 1, d))


def _final_norm_kernel(x_ref, g_ref, o_ref):
    x = x_ref[...]
    o_ref[...] = x * lax.rsqrt(jnp.mean(x * x, axis=-1, keepdims=True) + EPS) * g_ref[...]


def final_norm(xc, g, seq):
    d = xc.shape[1]
    return pl.pallas_call(
        _final_norm_kernel,
        grid=(seq // ROW_BLOCK,),
        in_specs=[pl.BlockSpec((ROW_BLOCK, d), lambda i: (i, 0)),
                  pl.BlockSpec((1, d), lambda i: (0, 0))],
        out_specs=pl.BlockSpec((ROW_BLOCK, d), lambda i: (i, 0)),
        out_shape=jax.ShapeDtypeStruct((seq, d), F32),
        compiler_params=_params("parallel"),
        name="final_norm",
    )(xc, g.reshape(1, d))


def _mm_tiles(t, n, wide=False):
    tm = _pick(t, (1280, 1024, 768, 640, 512, 384, 256))
    tn = _pick(n, ((1024,) if wide else ()) + (512, 384, 256, 128))
    return tm, tn


def _w_spec(k, tn, layer, col0=0):
    return pl.BlockSpec((None, k, tn), lambda i, j: (layer, 0, col0 + j))


def _up_kernel(h_ref, w1_ref, w3_ref, o_ref):
    h = h_ref[...]
    a = _dot(h, w1_ref[...])
    o_ref[...] = (_silu(a) * _dot(h, w3_ref[...])).astype(o_ref.dtype)


def swiglu_up(h, w1, w3, layer):
    t, d = h.shape
    n = w1.shape[2]
    tm, tn = _mm_tiles(t, n)
    return pl.pallas_call(
        _up_kernel,
        grid=(t // tm, n // tn),
        in_specs=[pl.BlockSpec((tm, d), lambda i, j: (i, 0)),
                  _w_spec(d, tn, layer), _w_spec(d, tn, layer)],
        out_specs=pl.BlockSpec((tm, tn), lambda i, j: (i, j)),
        out_shape=jax.ShapeDtypeStruct((t, n), BF16),
        compiler_params=_params("parallel", "arbitrary"),
        name="swiglu_up",
    )(h, w1, w3)


def _down_kernel(*refs, coef, tm, seq, n_latent_last):
    split = n_latent_last is not None
    if split:
        a_ref, w_ref, x_ref, c_ref, gate_ref, o_ref = refs
    else:
        a_ref, w_ref, x_ref, gate_ref, o_ref = refs
    i = pl.program_id(0)
    rows = i * tm + lax.broadcasted_iota(jnp.int32, (tm, 1), 0)
    gate = jnp.where(rows >= seq, gate_ref[1:2, :], gate_ref[0:1, :])
    upd = (coef * gate) * _dot(a_ref[...], w_ref[...])
    if not split:
        o_ref[...] = x_ref[...] + upd
        return
    last = pl.num_programs(0) - 1

    @pl.when(i < last)
    def _():
        o_ref[...] = x_ref[...] + upd

    @pl.when(i == last)
    def _():
        o_ref[:n_latent_last, :] = x_ref[:n_latent_last, :] + upd[:n_latent_last, :]
        o_ref[n_latent_last:, :] = c_ref[...] + upd[n_latent_last:, :]


def down_residual(a, w, layer, res, gate, coef, seq):
    t, k = a.shape
    n = w.shape[2]
    tm, tn = _mm_tiles(t, n)
    split = isinstance(res, tuple)
    res_spec = pl.BlockSpec((tm, tn), lambda i, j: (i, j))
    if split:
        ctx_len = res[1].shape[0]
        n_latent_last = tm - ctx_len
        assert res[0].shape[0] == seq and seq + ctx_len == t and 0 < n_latent_last and n_latent_last % 8 == 0
        res_specs = [res_spec, pl.BlockSpec((ctx_len, tn), lambda i, j: (0, j))]
    else:
        res, n_latent_last = (res,), None
        res_specs = [res_spec]
    return pl.pallas_call(
        functools.partial(_down_kernel, coef=coef, tm=tm, seq=seq, n_latent_last=n_latent_last),
        grid=(t // tm, n // tn),
        in_specs=[pl.BlockSpec((tm, k), lambda i, j: (i, 0)), _w_spec(k, tn, layer)] + res_specs
                 + [pl.BlockSpec((2, tn), lambda i, j: (0, j))],
        out_specs=pl.BlockSpec((tm, tn), lambda i, j: (i, j)),
        out_shape=jax.ShapeDtypeStruct((t, n), F32),
        input_output_aliases={} if split else {2: 0},
        compiler_params=_params("parallel", "arbitrary"),
        name="down_residual",
    )(a, w, *res, gate)


def _proj_kernel(h_ref, w_ref, b_ref, o_ref, *, sigmoid):
    acc = _dot(h_ref[...], w_ref[...]) + b_ref[...]
    if sigmoid:
        acc = jax.nn.sigmoid(acc)
    o_ref[...] = acc.astype(o_ref.dtype)


def proj(h, w, b, layer, col0, n, out_dtype, sigmoid=False, name="proj"):
    t, d = h.shape
    tm, tn = _mm_tiles(t, n, wide=True)
    while col0 % tn:
        tn //= 2
    assert tn % LANES == 0 and n % tn == 0
    nl, nw = b.shape
    return pl.pallas_call(
        functools.partial(_proj_kernel, sigmoid=sigmoid),
        grid=(t // tm, n // tn),
        in_specs=[pl.BlockSpec((tm, d), lambda i, j: (i, 0)),
                  _w_spec(d, tn, layer, col0 // tn),
                  pl.BlockSpec((None, 1, tn), lambda i, j: (layer, 0, col0 // tn + j))],
        out_specs=pl.BlockSpec((tm, tn), lambda i, j: (i, j)),
        out_shape=jax.ShapeDtypeStruct((t, n), out_dtype),
        compiler_params=_params("parallel", "arbitrary"),
        name=name,
    )(h, w, b.reshape(nl, 1, nw))


def _merge_kernel(ys_ref, ym_ref, yf_ref, ws_ref, wm_ref, wf_ref, gs_ref, gm_ref, gf_ref, o_ref):
    acc = gs_ref[...].astype(F32) * _dot(ys_ref[...], ws_ref[...])
    acc = acc + gm_ref[...].astype(F32) * _dot(ym_ref[...], wm_ref[...])
    acc = acc + gf_ref[...].astype(F32) * _dot(yf_ref[...], wf_ref[...])
    o_ref[...] = acc.astype(o_ref.dtype)


def merge(y_ssd, y_mlp, y_fft, w_ssd, w_mlp, w_fft, layer, gates):
    t = y_ssd.shape[0]
    n = w_ssd.shape[2]
    tm, tn = _mm_tiles(t, n)
    nb = n // tn
    row = lambda k: pl.BlockSpec((tm, k), lambda i, j: (i, 0))
    col = lambda k: _w_spec(k, tn, layer)
    gate = lambda b: pl.BlockSpec((tm, tn), lambda i, j: (i, b * nb + j))
    return pl.pallas_call(
        _merge_kernel,
        grid=(t // tm, nb),
        in_specs=[row(y_ssd.shape[1]), row(y_mlp.shape[1]), row(y_fft.shape[1]),
                  col(w_ssd.shape[1]), col(w_mlp.shape[1]), col(w_fft.shape[1]),
                  gate(0), gate(1), gate(2)],
        out_specs=pl.BlockSpec((tm, tn), lambda i, j: (i, j)),
        out_shape=jax.ShapeDtypeStruct((t, n), BF16),
        compiler_params=_params("parallel", "arbitrary"),
        name="merge",
    )(y_ssd, y_mlp, y_fft, w_ssd, w_mlp, w_fft, gates, gates, gates)


def _conv_kernel(u_ref, w_ref, b_ref, o_ref, *, nx, ctx_len):
    u = u_ref[...].astype(F32)
    rb = u.shape[0]
    width = jnp.where(pl.program_id(0) >= nx, ctx_len, GRID_W)
    pos = lax.broadcasted_iota(jnp.int32, (rb, 1), 0) & (width - 1)
    y = jnp.broadcast_to(b_ref[...], u.shape)
    for k in range(D_CONV):
        off = k - D_CONV // 2
        shifted = u if off == 0 else pltpu.roll(u, (-off) % rb, 0)
        valid = (pos + off >= 0) & (pos + off < width)
        y = y + jnp.where(valid, shifted, 0.0) * w_ref[k:k + 1, :]
    o_ref[...] = _silu(y).astype(o_ref.dtype)


def conv_silu(zx, col0, conv_w, conv_b, seq, ctx_len):
    t = zx.shape[0]
    c = conv_w.shape[1]
    tc = next(v for v in (1024, 512, 256, 128) if c % v == 0 and col0 % v == 0)
    assert ctx_len == ROW_BLOCK and ROW_BLOCK % GRID_W == 0
    assert GRID_W & (GRID_W - 1) == 0 and ctx_len & (ctx_len - 1) == 0
    nx = seq // ROW_BLOCK
    return pl.pallas_call(
        functools.partial(_conv_kernel, nx=nx, ctx_len=ctx_len),
        grid=(t // ROW_BLOCK, c // tc),
        in_specs=[pl.BlockSpec((ROW_BLOCK, tc), lambda i, j: (i, col0 // tc + j)),
                  pl.BlockSpec((D_CONV, tc), lambda i, j: (0, j)),
                  pl.BlockSpec((1, tc), lambda i, j: (0, j))],
        out_specs=pl.BlockSpec((ROW_BLOCK, tc), lambda i, j: (i, j)),
        out_shape=jax.ShapeDtypeStruct((t, c), BF16),
        compiler_params=_params("parallel", "arbitrary"),
        name="conv_silu",
    )(zx, conv_w, conv_b.reshape(1, c))


LOG2E = float(np.log2(np.e))


def _decay_kernel(raw_ref, bias_ref, a_ref, cs_ref, src_ref, wst_ref, etot_ref, *, heads, chunks):
    qi = lax.broadcasted_iota(jnp.int32, (CHUNK, CHUNK), 0)
    si = lax.broadcasted_iota(jnp.int32, (CHUNK, CHUNK), 1)
    lower, upper = (si <= qi).astype(F32), (si >= qi).astype(F32)
    fwd_col = lax.broadcasted_iota(jnp.int32, (CHUNK, LANES), 1) < heads
    for c in range(chunks):
        rows = slice(c * CHUNK, (c + 1) * CHUNK)
        v = raw_ref[rows, :] + bias_ref[...]
        dt = jnp.maximum(v, 0.0) + jnp.log1p(jnp.exp(-jnp.abs(v)))
        a = dt * a_ref[...]
        pre, suf = _dot(lower, a, HI), _dot(upper, a, HI)
        cs = jnp.where(fwd_col, pre, suf)
        tot = jnp.where(fwd_col[0:1, :], pre[CHUNK - 1:CHUNK, :], suf[0:1, :])
        cs_ref[rows, :] = cs * LOG2E
        src_ref[rows, :] = ((cs - jnp.log(dt)) * LOG2E).T
        wst_ref[rows, :] = (jnp.exp(tot - cs) * dt).T
        etot_ref[c * 8:(c + 1) * 8, :] = jnp.broadcast_to(jnp.exp(tot), (8, LANES))


def ssd_decays(dt_raw, dt_bias, a_neg):
    t = dt_raw.shape[0]
    nc = t // CHUNK
    heads = dt_bias.shape[0] // 2
    chunks = _pick(nc, (10, 6, 5, 4, 3, 2, 1))
    pad = lambda vec: jnp.pad(vec.reshape(1, -1), ((0, 0), (0, LANES - vec.shape[0])))
    big = pl.BlockSpec((chunks * CHUNK, LANES), lambda i: (i, 0))
    vec = pl.BlockSpec((1, LANES), lambda i: (0, 0))
    full = jax.ShapeDtypeStruct((t, LANES), F32)
    return pl.pallas_call(
        functools.partial(_decay_kernel, heads=heads, chunks=chunks),
        grid=(nc // chunks,),
        in_specs=[big, vec, vec],
        out_specs=[big, big, big, pl.BlockSpec((chunks * 8, LANES), lambda i: (i, 0))],
        out_shape=[full, full, full, jax.ShapeDtypeStruct((nc * 8, LANES), F32)],
        compiler_params=_params("parallel"),
        name="ssd_decays",
    )(dt_raw, pad(dt_bias), pad(a_neg))


def _ssd_kernel(*refs, direction, heads, d_ssd, d_state, final):
    if final:
        x_ref, cs_ref, src_ref, wst_ref, etot_ref, yf_ref, z_ref, dsk_ref, g_ref, o_ref, state_ref = refs
    else:
        x_ref, cs_ref, src_ref, wst_ref, etot_ref, o_ref, state_ref = refs
    gp = d_ssd // SSD_GROUPS
    r_heads = gp // SSD_HEAD_DIM
    gn = SSD_GROUPS * d_state

    @pl.when(pl.program_id(0) == 0)
    def _():
        state_ref[...] = jnp.zeros_like(state_ref)

    qi = lax.broadcasted_iota(jnp.int32, (CHUNK, CHUNK), 0)
    si = lax.broadcasted_iota(jnp.int32, (CHUNK, CHUNK), 1)
    causal = (si <= qi) if direction == 0 else (si >= qi)
    cs = cs_ref[...]
    src_t = src_ref[...]
    wst_t = wst_ref[...]
    etot = etot_ref[0:1, :]
    lo = lax.broadcasted_iota(jnp.int32, (CHUNK, LANES), 1) < SSD_HEAD_DIM
    lo_row = lo[0:1, :]

    for g in range(SSD_GROUPS):
        h0 = direction * heads + g * r_heads
        bm = x_ref[:, d_ssd + g * d_state:d_ssd + (g + 1) * d_state]
        cm = x_ref[:, d_ssd + gn + g * d_state:d_ssd + gn + (g + 1) * d_state]
        cb = lax.dot_general(cm, bm, (((1,), (1,)), ((), ())), preferred_element_type=F32)
        bt = bm.astype(F32).T
        y_state = _dot(cm, state_ref[:, g * gp:(g + 1) * gp].astype(BF16))
        ys = []
        for k in range(r_heads // 2):
            ps = slice(g * gp + k * LANES, g * gp + (k + 1) * LANES)
            xd = x_ref[:, ps].astype(F32)
            y_acc = st_acc = ecs = None
            for half in range(2):
                h = h0 + 2 * k + half
                csq = jnp.broadcast_to(cs[:, h:h + 1], (CHUNK, CHUNK))
                m = (cb * jnp.exp2(jnp.where(causal, csq - src_t[h:h + 1, :], -1e30))).astype(BF16)
                xh = (jnp.where(lo, xd, 0.0) if half == 0 else jnp.where(lo, 0.0, xd)).astype(BF16)
                btw = (bt * wst_t[h:h + 1, :]).astype(BF16)
                py, pst = _dot(m, xh), _dot(btw, xh)
                e = jnp.exp2(csq)
                if half == 0:
                    y_acc, st_acc, ecs = py, pst, e
                else:
                    y_acc, st_acc, ecs = y_acc + py, st_acc + pst, jnp.where(lo, ecs, e)
            ys.append(y_acc + ecs * y_state[:, k * LANES:(k + 1) * LANES])
            h = h0 + 2 * k
            decay = jnp.where(lo_row, jnp.broadcast_to(etot[:, h:h + 1], (1, LANES)),
                              jnp.broadcast_to(etot[:, h + 1:h + 2], (1, LANES)))
            state_ref[:, ps] = state_ref[:, ps] * decay + st_acc

        gs = slice(g * gp, (g + 1) * gp)
        y = ys[0] if len(ys) == 1 else jnp.concatenate(ys, axis=1)
        if final:
            y = (yf_ref[:, gs] + y + dsk_ref[:, gs] * x_ref[:, gs].astype(F32)) * _silu(z_ref[:, gs].astype(F32))
            y = y * lax.rsqrt(jnp.mean(y * y, axis=-1, keepdims=True) + EPS) * g_ref[:, gs]
        o_ref[:, gs] = y.astype(o_ref.dtype)


def ssd_mixer(xbc, zx, dt_raw, dt_bias, a_neg, d_skip_x, norm_g, seq, d_ssd, d_state):
    t, c_xbc = xbc.shape
    nc = t // CHUNK
    nxc = seq // CHUNK
    ncc = nc - nxc
    heads = dt_bias.shape[0] // 2
    gp = d_ssd // SSD_GROUPS
    assert d_state == LANES and (gp // SSD_HEAD_DIM) % 2 == 0 and gp % LANES == 0 and 2 * heads <= LANES
    fwd = lambda j: jnp.where(j < ncc, nxc + j, j - ncc)
    bwd = lambda j: nc - 1 - j
    vec = lambda n: pl.BlockSpec((1, n), lambda j: (0, 0))
    decays = ssd_decays(dt_raw, dt_bias, a_neg)

    def call(direction, order, final, extra_in, extra_specs, out_dtype):
        per_chunk = pl.BlockSpec((CHUNK, LANES), lambda j: (order(j), 0))
        return pl.pallas_call(
            functools.partial(_ssd_kernel, direction=direction, heads=heads, d_ssd=d_ssd, d_state=d_state,
                              final=final),
            grid=(nc,),
            in_specs=[pl.BlockSpec((CHUNK, c_xbc), lambda j: (order(j), 0)),
                      per_chunk, per_chunk, per_chunk,
                      pl.BlockSpec((8, LANES), lambda j: (order(j), 0))] + extra_specs,
            out_specs=pl.BlockSpec((CHUNK, d_ssd), lambda j: (order(j), 0)),
            out_shape=jax.ShapeDtypeStruct((t, d_ssd), out_dtype),
            scratch_shapes=[pltpu.VMEM((d_state, d_ssd), F32)],
            compiler_params=_params("arbitrary"),
            name="ssd_bwd_out" if final else "ssd_fwd",
        )(xbc, *decays, *extra_in)

    y_f = call(0, fwd, False, [], [], F32)
    blk = pl.BlockSpec((CHUNK, d_ssd), lambda j: (bwd(j), 0))
    return call(1, bwd, True,
                [y_f, zx, d_skip_x.reshape(1, d_ssd), norm_g.reshape(1, d_ssd)],
                [blk, blk, vec(d_ssd), vec(d_ssd)], BF16)


def _sgu_kernel(u_ref, v_ref, g_ref, w_ref, bt_ref, o_ref, *, groups):
    u = _gelu_tanh(u_ref[...].astype(F32))
    v = _gelu_tanh(v_ref[...].astype(F32))
    v = (v * lax.rsqrt(jnp.mean(v * v, axis=-1, keepdims=True) + EPS) * g_ref[...]).astype(BF16)
    for g in range(groups):
        sl = slice(g * GROUP_DIM, (g + 1) * GROUP_DIM)
        vm = _dot(w_ref[g], v[:, sl]) + bt_ref[:, g:g + 1]
        o_ref[:, sl] = (u[:, sl] * vm).astype(o_ref.dtype)


def sgu(uvf, norm_g, w_s, b_s):
    t = uvf.shape[0]
    groups = w_s.shape[0]
    dm = groups * GROUP_DIM
    return pl.pallas_call(
        functools.partial(_sgu_kernel, groups=groups),
        grid=(t // CHUNK,),
        in_specs=[pl.BlockSpec((CHUNK, dm), lambda c: (c, 0)),
                  pl.BlockSpec((CHUNK, dm), lambda c: (c, 1)),
                  pl.BlockSpec((1, dm), lambda c: (0, 0)),
                  pl.BlockSpec((groups, CHUNK, CHUNK), lambda c: (0, 0, 0)),
                  pl.BlockSpec((CHUNK, groups), lambda c: (0, 0))],
        out_specs=pl.BlockSpec((CHUNK, dm), lambda c: (c, 0)),
        out_shape=jax.ShapeDtypeStruct((t, dm), BF16),
        compiler_params=_params("parallel"),
        name="sgu",
    )(uvf, uvf, norm_g.reshape(1, dm), w_s.astype(BF16), b_s.T)


def _channel_dft(f, cs_tab):
    gr, gs = [], []
    for g in range(f.shape[1] // GROUP_DIM):
        p = _dot(f[:, g * GROUP_DIM:(g + 1) * GROUP_DIM], cs_tab)
        gr.append(p[:, :GROUP_DIM])
        gs.append(p[:, GROUP_DIM:])
    cat = lambda v: (v[0] if len(v) == 1 else jnp.concatenate(v, axis=1)).astype(BF16)
    return cat(gr), cat(gs)


def _fft_a_kernel(f_ref, m_ref, cs_ref, z_ref, *, n1):
    gr, gs = _channel_dft(f_ref[...], cs_ref[...])
    m = m_ref[...]
    z_ref[...] = (_dot(m[:, :n1], gr) + _dot(m[:, n1:], gs)).astype(z_ref.dtype)


def _fft_b_kernel(zr_ref, zi_ref, c_ref, s_ref, o_ref, *, scale):
    acc = _dot(c_ref[...], zr_ref[...]) - _dot(s_ref[...], zi_ref[...])
    o_ref[...] = (acc * scale).astype(o_ref.dtype)


def _fft_ctx_kernel(f_ref, c_ref, s_ref, cs_ref, o_ref, *, scale):
    gr, gs = _channel_dft(f_ref[...], cs_ref[...])
    acc = _dot(c_ref[...], gr) - _dot(s_ref[...], gs)
    o_ref[...] = (acc * scale).astype(o_ref.dtype)


def _cos_sin(n):
    ang = 2.0 * np.pi * (np.outer(np.arange(n), np.arange(n)) % n) / n
    return np.cos(ang), np.sin(ang)


def fourier_mix(uvf, col0, d_fft, seq, ctx_len):
    n2 = CHUNK
    n1 = seq // n2
    assert n1 * n2 == seq and n1 % 16 == 0 and d_fft % GROUP_DIM == 0
    cc, sc = _cos_sin(GROUP_DIM)
    cs_tab = jnp.asarray(np.concatenate([cc, sc], axis=1), BF16)
    c1, s1 = _cos_sin(n1)
    ang = 2.0 * np.pi * np.outer(np.arange(n2), np.arange(n1)) / seq
    ct, st = (jnp.asarray(v[:, :, None], F32) for v in (np.cos(ang), np.sin(ang)))
    c1, s1 = jnp.asarray(c1[None], F32), jnp.asarray(s1[None], F32)
    mc = c1 * ct - s1 * st
    ms = s1 * ct + c1 * st
    m_tab = jnp.concatenate([jnp.concatenate([mc, -ms], axis=2),
                             jnp.concatenate([ms, mc], axis=2)], axis=1).astype(BF16)
    c2, s2 = (jnp.asarray(v, BF16) for v in _cos_sin(n2))

    f_t = uvf[:seq, col0:col0 + d_fft].reshape(n1, n2, d_fft).transpose(1, 0, 2)
    z = pl.pallas_call(
        functools.partial(_fft_a_kernel, n1=n1),
        grid=(n2,),
        in_specs=[pl.BlockSpec((None, n1, d_fft), lambda a: (a, 0, 0)),
                  pl.BlockSpec((None, 2 * n1, 2 * n1), lambda a: (a, 0, 0)),
                  pl.BlockSpec((GROUP_DIM, 2 * GROUP_DIM), lambda a: (0, 0))],
        out_specs=pl.BlockSpec((None, 2 * n1, d_fft), lambda a: (a, 0, 0)),
        out_shape=jax.ShapeDtypeStruct((n2, 2 * n1, d_fft), BF16),
        compiler_params=_params("parallel"),
        name="fft_stage_a",
    )(f_t, m_tab, cs_tab)

    width = n1 * d_fft
    tw = _pick(width, (8192, 4096, 2048, 1024, 512, 256, 128))
    nb = width // tw
    z2 = z.reshape(n2, 2 * width)
    tab = lambda: pl.BlockSpec((n2, n2), lambda j: (0, 0))
    y_x = pl.pallas_call(
        functools.partial(_fft_b_kernel, scale=float(1.0 / np.sqrt(seq * GROUP_DIM))),
        grid=(nb,),
        in_specs=[pl.BlockSpec((n2, tw), lambda j: (0, j)),
                  pl.BlockSpec((n2, tw), lambda j: (0, nb + j)),
                  tab(), tab()],
        out_specs=pl.BlockSpec((n2, tw), lambda j: (0, j)),
        out_shape=jax.ShapeDtypeStruct((n2, width), BF16),
        compiler_params=_params("parallel"),
        name="fft_stage_b",
    )(z2, z2, c2, s2).reshape(seq, d_fft)

    assert ctx_len == ROW_BLOCK and col0 % d_fft == 0
    cl, sl = (jnp.asarray(v, BF16) for v in _cos_sin(ctx_len))
    ctab = lambda: pl.BlockSpec((ctx_len, ctx_len), lambda j: (0, 0))
    y_c = pl.pallas_call(
        functools.partial(_fft_ctx_kernel, scale=float(1.0 / np.sqrt(ctx_len * GROUP_DIM))),
        grid=(1,),
        in_specs=[pl.BlockSpec((ctx_len, d_fft), lambda j: (seq // ctx_len, col0 // d_fft)),
                  ctab(), ctab(),
                  pl.BlockSpec((GROUP_DIM, 2 * GROUP_DIM), lambda j: (0, 0))],
        out_specs=pl.BlockSpec((ctx_len, d_fft), lambda j: (0, 0)),
        out_shape=jax.ShapeDtypeStruct((ctx_len, d_fft), BF16),
        compiler_params=_params("arbitrary"),
        name="fft_ctx",
    )(uvf, cl, sl, cs_tab)
    return jnp.concatenate([y_x, y_c], axis=0)


def kernel(x, c, ctx, c_ctx, ada_w1, ada_w2, ada_b, norm_ffn1, f1_w1, f1_w3, f1_w2, norm_mix, w_in, b_in,
           conv_w, conv_b, a_log, dt_bias, d_skip, norm_ssd, sgu_norm, sgu_w, sgu_b,
           w_br_ssd, w_br_mlp, w_br_fft, w_out, norm_ffn2, f2_w1, f2_w3, f2_w2, norm_final):
    batch, seq, d = x.shape
    ctx_len = ctx.shape[1]
    depth = ada_w1.shape[0]
    assert batch == 1 and seq % ROW_BLOCK == 0
    heads = a_log.shape[-1]
    d_ssd = norm_ssd.shape[-1]
    c_xbc = conv_w.shape[-1]
    d_state = (c_xbc - d_ssd) // (2 * SSD_GROUPS)
    d_mlp = sgu_norm.shape[-1]
    d_fft = w_br_fft.shape[1]
    assert heads * SSD_HEAD_DIM == d_ssd and d_mlp == d_fft
    off_xbc = d_ssd
    off_dt = off_xbc + c_xbc
    off_u = off_dt + 2 * heads
    off_gate = off_u + 2 * d_mlp + d_fft

    cv = jnp.zeros((8, d), F32).at[0].set(c[0]).at[1].set(c_ctx)
    mods = ada_mods(cv, ada_w1, ada_w2, ada_b)[:, 0:2, :].reshape(depth, 2, N_MOD, d)
    bf = lambda w: w.astype(BF16)
    f1_w1, f1_w3, f1_w2, f2_w1, f2_w3, f2_w2 = map(bf, (f1_w1, f1_w3, f1_w2, f2_w1, f2_w3, f2_w2))
    w_br_ssd, w_br_mlp, w_br_fft, w_out = map(bf, (w_br_ssd, w_br_mlp, w_br_fft, w_out))
    dt_pad = LANES - 2 * heads
    w_zx, b_zx = bf(w_in[:, :, :off_dt]), b_in[:, :off_dt]
    w_dt = bf(jnp.pad(w_in[:, :, off_dt:off_u], ((0, 0), (0, 0), (0, dt_pad))))
    b_dt = jnp.pad(b_in[:, off_dt:off_u], ((0, 0), (0, dt_pad)))
    w_rest, b_rest = bf(w_in[:, :, off_u:]), b_in[:, off_u:]
    n_uvf = off_gate - off_u

    xc = (x[0], ctx[0])
    for i in range(depth):
        sh1, sc1, g1, shm, scm, gm, sh2, sc2, g2 = (mods[i, :, k, :] for k in range(N_MOD))

        h = norm_mod(xc, norm_ffn1[i], sh1, sc1, seq)
        xc = down_residual(swiglu_up(h, f1_w1, f1_w3, i), f1_w2, i, xc, g1, 0.5, seq)

        h = norm_mod(xc, norm_mix[i], shm, scm, seq)
        zx = proj(h, w_zx, b_zx, i, 0, off_dt, BF16, name="proj_zx")
        dt_raw = proj(h, w_dt, b_dt, i, 0, LANES, F32, name="proj_dt")
        uvf = proj(h, w_rest, b_rest, i, 0, n_uvf, BF16, name="proj_uvf")
        gates = proj(h, w_rest, b_rest, i, n_uvf, 3 * d, BF16, sigmoid=True, name="proj_gates")

        xbc = conv_silu(zx, off_xbc, conv_w[i], conv_b[i], seq, ctx_len)
        a_neg = -jnp.exp(a_log[i].astype(F32)).reshape(-1)
        y_ssd = ssd_mixer(xbc, zx, dt_raw, dt_bias[i].reshape(-1), a_neg,
                          jnp.repeat(d_skip[i], SSD_HEAD_DIM), norm_ssd[i], seq, d_ssd, d_state)
        y_mlp = sgu(uvf, sgu_norm[i], sgu_w[i], sgu_b[i])
        y_fft = fourier_mix(uvf, 2 * d_mlp, d_fft, seq, ctx_len)

        merged = merge(y_ssd, y_mlp, y_fft, w_br_ssd, w_br_mlp, w_br_fft, i, gates)
        xc = down_residual(merged, w_out, i, xc, gm, 1.0, seq)

        h = norm_mod(xc, norm_ffn2[i], sh2, sc2, seq)
        xc = down_residual(swiglu_up(h, f2_w1, f2_w3, i), f2_w2, i, xc, g2, 0.5, seq)

    return final_norm(xc, norm_final, seq).reshape(1, seq, d)
```

```python
import functools

import numpy as np
import jax
import jax.numpy as jnp
from jax import lax
from jax.experimental import pallas as pl
from jax.experimental.pallas import tpu as pltpu

F32 = jnp.float32
BF16 = jnp.bfloat16
HI = lax.Precision.HIGHEST

EPS = 1e-6
GRID_W = 64
SSD_GROUPS = 4
SSD_HEAD_DIM = 64
CHUNK = 128
D_CONV = 5
GROUP_DIM = 128
N_MOD = 9
LANES = 128
ROW_BLOCK = 256
VMEM_LIMIT = 56 * 1024 * 1024


def _pick(n, cands):
    for c in cands:
        if n % c == 0:
            return c
    raise ValueError(f"no tile for {n} in {cands}")


def _params(*sem):
    return pltpu.CompilerParams(dimension_semantics=sem, vmem_limit_bytes=VMEM_LIMIT)


def _silu(v):
    return v * jax.nn.sigmoid(v)


def _gelu_tanh(v):
    return v * (0.5 * (1.0 + jnp.tanh(np.sqrt(2.0 / np.pi).astype(np.float32) * (v + 0.044715 * (v * v * v)))))


def _dot(a, b, precision=None):
    return jnp.dot(a, b, preferred_element_type=F32, precision=precision)


def _ada_kernel(cv_ref, w1_ref, w2_ref, b_ref, o_ref):
    t = _dot(_silu(cv_ref[...]), w1_ref[0], HI)
    o_ref[0] = _dot(t, w2_ref[0], HI) + b_ref[0]


def ada_mods(cv, w1, w2, b):
    nl, d, r = w1.shape
    n = w2.shape[-1]
    tn = _pick(n, (4608, 4096, 2304, 2048, 1024, 512, 256, 128))
    return pl.pallas_call(
        _ada_kernel,
        grid=(nl, n // tn),
        in_specs=[pl.BlockSpec((8, d), lambda l, j: (0, 0)),
                  pl.BlockSpec((1, d, r), lambda l, j: (l, 0, 0)),
                  pl.BlockSpec((1, r, tn), lambda l, j: (l, 0, j)),
                  pl.BlockSpec((1, 1, tn), lambda l, j: (l, 0, j))],
        out_specs=pl.BlockSpec((1, 8, tn), lambda l, j: (l, 0, j)),
        out_shape=jax.ShapeDtypeStruct((nl, 8, n), F32),
        compiler_params=_params("arbitrary", "arbitrary"),
        name="ada_mods",
    )(cv, w1, w2, b.reshape(nl, 1, n))


def _lane_partial_sumsq(x):
    acc = None
    for k in range(x.shape[1] // LANES):
        v = x[:, k * LANES:(k + 1) * LANES]
        acc = v * v if acc is None else acc + v * v
    return acc


def _rstd(ssq_ref, d):
    return lax.rsqrt(jnp.sum(ssq_ref[...], axis=-1, keepdims=True) * (1.0 / d) + EPS)


def _scale_first_kernel(x_ref, c_ref, g_ref, xg_ref, ssq_ref, *, nx):
    def emit(src_ref):
        x = src_ref[...]
        xg_ref[...] = (x * g_ref[0]).astype(xg_ref.dtype)
        ssq_ref[...] = _lane_partial_sumsq(x)

    pl.when(pl.program_id(0) < nx)(lambda: emit(x_ref))
    pl.when(pl.program_id(0) >= nx)(lambda: emit(c_ref))


def scale_first(x, ctx, gmul):
    seq, d = x.shape
    t = seq + ctx.shape[0]
    nx = seq // ROW_BLOCK
    return pl.pallas_call(
        functools.partial(_scale_first_kernel, nx=nx),
        grid=(t // ROW_BLOCK,),
        in_specs=[pl.BlockSpec((ROW_BLOCK, d), lambda i: (jnp.minimum(i, nx - 1), 0)),
                  pl.BlockSpec((ROW_BLOCK, d), lambda i: (jnp.maximum(i - nx, 0), 0)),
                  pl.BlockSpec((1, 1, d), lambda i: (jnp.where(i >= nx, 1, 0), 0, 0))],
        out_specs=[pl.BlockSpec((ROW_BLOCK, d), lambda i: (i, 0)),
                   pl.BlockSpec((ROW_BLOCK, LANES), lambda i: (i, 0))],
        out_shape=[jax.ShapeDtypeStruct((t, d), BF16), jax.ShapeDtypeStruct((t, LANES), F32)],
        compiler_params=_params("parallel"),
        name="scale_first",
    )(x, ctx, gmul.reshape(2, 1, d))


def _final_norm_kernel(x_ref, g_ref, o_ref):
    x = x_ref[...]
    o_ref[...] = x * lax.rsqrt(jnp.mean(x * x, axis=-1, keepdims=True) + EPS) * g_ref[...]


def final_norm(xc, g, seq):
    d = xc.shape[1]
    return pl.pallas_call(
        _final_norm_kernel,
        grid=(seq // ROW_BLOCK,),
        in_specs=[pl.BlockSpec((ROW_BLOCK, d), lambda i: (i, 0)),
                  pl.BlockSpec((1, d), lambda i: (0, 0))],
        out_specs=pl.BlockSpec((ROW_BLOCK, d), lambda i: (i, 0)),
        out_shape=jax.ShapeDtypeStruct((seq, d), F32),
        compiler_params=_params("parallel"),
        name="final_norm",
    )(xc, g.reshape(1, d))


def _mm_tiles(t, n, wide=False):
    tm = _pick(t, (1280, 1024, 768, 640, 512, 384, 256))
    tn = _pick(n, ((1024,) if wide else ()) + (512, 384, 256, 128))
    return tm, tn


def _w_spec(k, tn, layer, col0=0):
    return pl.BlockSpec((None, k, tn), lambda i, j: (layer, 0, col0 + j))


def _split_w_in_kernel(w_ref, zx_ref, dt_ref, rest_ref, *, off_dt, off_u):
    x = w_ref[...]
    zx_ref[...] = x[:, :off_dt].astype(BF16)
    lane = lax.broadcasted_iota(jnp.int32, (x.shape[0], LANES), 1)
    dt_ref[...] = jnp.where(lane < off_u - off_dt, x[:, off_dt:off_dt + LANES], 0.0).astype(BF16)
    rest_ref[...] = x[:, off_u:].astype(BF16)


def split_w_in(w_in, off_dt, off_u):
    nl, d, d_in = w_in.shape
    rows = _pick(d, (64, 32, 16))
    assert off_dt % LANES == 0 and off_u - off_dt <= LANES
    shapes = [(off_dt,), (LANES,), (d_in - off_u,)]
    return pl.pallas_call(
        functools.partial(_split_w_in_kernel, off_dt=off_dt, off_u=off_u),
        grid=(nl, d // rows),
        in_specs=[pl.BlockSpec((None, rows, d_in), lambda l, r: (l, r, 0))],
        out_specs=[pl.BlockSpec((None, rows, n), lambda l, r: (l, r, 0)) for (n,) in shapes],
        out_shape=[jax.ShapeDtypeStruct((nl, d, n), BF16) for (n,) in shapes],
        compiler_params=_params("parallel", "parallel"),
        name="split_w_in",
    )(w_in)


def _row_kind(vec_ref, tm, seq):
    rows = pl.program_id(0) * tm + lax.broadcasted_iota(jnp.int32, (tm, 1), 0)
    return jnp.where(rows >= seq, vec_ref[1:2, :], vec_ref[0:1, :])


def _shift_kernel(s_ref, w_ref, o_ref):
    o_ref[...] = _dot(s_ref[...], w_ref[...])


def shift_proj(shift, w, layer):
    d, n = w.shape[1], w.shape[2]
    tn = _pick(n, (1024, 512, 384, 256, 128))
    s16 = jnp.pad(shift, ((0, 14), (0, 0))).astype(BF16)
    return pl.pallas_call(
        _shift_kernel,
        grid=(1, n // tn),
        in_specs=[pl.BlockSpec((16, d), lambda i, j: (0, 0)), _w_spec(d, tn, layer)],
        out_specs=pl.BlockSpec((16, tn), lambda i, j: (0, j)),
        out_shape=jax.ShapeDtypeStruct((16, n), F32),
        compiler_params=_params("arbitrary", "arbitrary"),
        name="shift_proj",
    )(s16, w)[0:2]


def _up_kernel(xg_ref, ssq_ref, w1_ref, w3_ref, s1_ref, s3_ref, o_ref, *, tm, seq):
    xg = xg_ref[...]
    rstd = _rstd(ssq_ref, xg.shape[1])
    a = rstd * _dot(xg, w1_ref[...]) + _row_kind(s1_ref, tm, seq)
    b = rstd * _dot(xg, w3_ref[...]) + _row_kind(s3_ref, tm, seq)
    o_ref[...] = (_silu(a) * b).astype(o_ref.dtype)


def swiglu_up(xg, ssq, w1, w3, s1, s3, layer, seq):
    t, d = xg.shape
    n = w1.shape[2]
    tm, tn = _mm_tiles(t, n)
    vec = pl.BlockSpec((2, tn), lambda i, j: (0, j))
    return pl.pallas_call(
        functools.partial(_up_kernel, tm=tm, seq=seq),
        grid=(t // tm, n // tn),
        in_specs=[pl.BlockSpec((tm, d), lambda i, j: (i, 0)),
                  pl.BlockSpec((tm, LANES), lambda i, j: (i, 0)),
                  _w_spec(d, tn, layer), _w_spec(d, tn, layer), vec, vec],
        out_specs=pl.BlockSpec((tm, tn), lambda i, j: (i, j)),
        out_shape=jax.ShapeDtypeStruct((t, n), BF16),
        compiler_params=_params("parallel", "arbitrary"),
        name="swiglu_up",
    )(xg, ssq, w1, w3, s1, s3)


def _down_kernel(*refs, coef, tm, seq, n_latent_last, emit_next):
    split = n_latent_last is not None
    it = iter(refs)
    a_ref, w_ref, x_ref = next(it), next(it), next(it)
    c_ref = next(it) if split else None
    gate_ref = next(it)
    gn_ref = next(it) if emit_next else None
    o_ref = next(it)
    xg_ref, ssq_ref = (next(it), next(it)) if emit_next else (None, None)
    i, j = pl.program_id(0), pl.program_id(1)
    upd = (coef * _row_kind(gate_ref, tm, seq)) * _dot(a_ref[...], w_ref[...])
    gn = _row_kind(gn_ref, tm, seq) if emit_next else None

    def finish(new, rows):
        o_ref[rows, :] = new
        if emit_next:
            xg_ref[rows, :] = (new * gn[rows, :]).astype(xg_ref.dtype)
            part = _lane_partial_sumsq(new)

            @pl.when(j == 0)
            def _():
                ssq_ref[rows, :] = part

            @pl.when(j > 0)
            def _():
                ssq_ref[rows, :] = ssq_ref[rows, :] + part

    if not split:
        finish(x_ref[...] + upd, slice(None))
        return
    last = pl.num_programs(0) - 1

    @pl.when(i < last)
    def _():
        finish(x_ref[...] + upd, slice(None))

    @pl.when(i == last)
    def _():
        lat, cx = slice(0, n_latent_last), slice(n_latent_last, tm)
        finish(x_ref[lat, :] + upd[lat, :], lat)
        finish(c_ref[...] + upd[cx, :], cx)


def down_residual(a, w, layer, res, gate, coef, seq, next_gmul=None):
    t, k = a.shape
    n = w.shape[2]
    tm, tn = _mm_tiles(t, n)
    split = isinstance(res, tuple)
    emit_next = next_gmul is not None
    tile = pl.BlockSpec((tm, tn), lambda i, j: (i, j))
    vec = pl.BlockSpec((2, tn), lambda i, j: (0, j))
    if split:
        ctx_len = res[1].shape[0]
        n_latent_last = tm - ctx_len
        assert res[0].shape[0] == seq and seq + ctx_len == t and 0 < n_latent_last and n_latent_last % 16 == 0
        res_specs = [tile, pl.BlockSpec((ctx_len, tn), lambda i, j: (0, j))]
    else:
        res, n_latent_last = (res,), None
        res_specs = [tile]
    out_specs, out_shape = [tile], [jax.ShapeDtypeStruct((t, n), F32)]
    if emit_next:
        out_specs += [tile, pl.BlockSpec((tm, LANES), lambda i, j: (i, 0))]
        out_shape += [jax.ShapeDtypeStruct((t, n), BF16), jax.ShapeDtypeStruct((t, LANES), F32)]
    out = pl.pallas_call(
        functools.partial(_down_kernel, coef=coef, tm=tm, seq=seq, n_latent_last=n_latent_last,
                          emit_next=emit_next),
        grid=(t // tm, n // tn),
        in_specs=[pl.BlockSpec((tm, k), lambda i, j: (i, 0)), _w_spec(k, tn, layer)] + res_specs
                 + [vec] * (1 + emit_next),
        out_specs=out_specs,
        out_shape=out_shape,
        input_output_aliases={} if split else {2: 0},
        compiler_params=_params("parallel", "arbitrary"),
        name="down_residual",
    )(a, w, *res, gate, *([next_gmul] if emit_next else []))
    return out if emit_next else out[0]


def _proj_kernel(*refs, tm, seq, sigmoid, with_dt):
    if with_dt:
        xg_ref, ssq_ref, w_ref, b_ref, wdt_ref, bdt_ref, o_ref, dt_ref = refs
    else:
        xg_ref, ssq_ref, w_ref, b_ref, o_ref = refs
    xg = xg_ref[...]
    rstd = _rstd(ssq_ref, xg.shape[1])
    acc = rstd * _dot(xg, w_ref[...]) + _row_kind(b_ref, tm, seq)
    if sigmoid:
        acc = jax.nn.sigmoid(acc)
    o_ref[...] = acc.astype(o_ref.dtype)
    if with_dt:
        @pl.when(pl.program_id(1) == 0)
        def _():
            dt_ref[...] = rstd * _dot(xg, wdt_ref[...]) + _row_kind(bdt_ref, tm, seq)


def proj(xg, ssq, w, b, layer, col0, n, seq, out_dtype, sigmoid=False, dt=None, name="proj"):
    t, d = xg.shape
    tm, tn = _mm_tiles(t, n, wide=dt is None)
    while col0 % tn:
        tn //= 2
    assert tn % LANES == 0 and n % tn == 0
    in_specs = [pl.BlockSpec((tm, d), lambda i, j: (i, 0)),
                pl.BlockSpec((tm, LANES), lambda i, j: (i, 0)),
                _w_spec(d, tn, layer, col0 // tn),
                pl.BlockSpec((2, tn), lambda i, j: (0, col0 // tn + j))]
    out_specs = [pl.BlockSpec((tm, tn), lambda i, j: (i, j))]
    out_shape = [jax.ShapeDtypeStruct((t, n), out_dtype)]
    args = [xg, ssq, w, b]
    if dt is not None:
        in_specs += [pl.BlockSpec((None, d, LANES), lambda i, j: (layer, 0, 0)),
                     pl.BlockSpec((2, LANES), lambda i, j: (0, 0))]
        out_specs.append(pl.BlockSpec((tm, LANES), lambda i, j: (i, 0)))
        out_shape.append(jax.ShapeDtypeStruct((t, LANES), F32))
        args += list(dt)
    out = pl.pallas_call(
        functools.partial(_proj_kernel, tm=tm, seq=seq, sigmoid=sigmoid, with_dt=dt is not None),
        grid=(t // tm, n // tn),
        in_specs=in_specs,
        out_specs=out_specs,
        out_shape=out_shape,
        compiler_params=_params("parallel", "arbitrary"),
        name=name,
    )(*args)
    return out if dt is not None else out[0]


def _merge_kernel(ys_ref, ym_ref, yf_ref, ws_ref, wm_ref, wf_ref, gs_ref, gm_ref, gf_ref, o_ref):
    acc = gs_ref[...].astype(F32) * _dot(ys_ref[...], ws_ref[...])
    acc = acc + gm_ref[...].astype(F32) * _dot(ym_ref[...], wm_ref[...])
    acc = acc + gf_ref[...].astype(F32) * _dot(yf_ref[...], wf_ref[...])
    o_ref[...] = acc.astype(o_ref.dtype)


def merge(y_ssd, y_mlp, y_fft, w_ssd, w_mlp, w_fft, layer, gates):
    t = y_ssd.shape[0]
    n = w_ssd.shape[2]
    tm, tn = _mm_tiles(t, n)
    nb = n // tn
    row = lambda k: pl.BlockSpec((tm, k), lambda i, j: (i, 0))
    col = lambda k: _w_spec(k, tn, layer)
    gate = lambda b: pl.BlockSpec((tm, tn), lambda i, j: (i, b * nb + j))
    return pl.pallas_call(
        _merge_kernel,
        grid=(t // tm, nb),
        in_specs=[row(y_ssd.shape[1]), row(y_mlp.shape[1]), row(y_fft.shape[1]),
                  col(w_ssd.shape[1]), col(w_mlp.shape[1]), col(w_fft.shape[1]),
                  gate(0), gate(1), gate(2)],
        out_specs=pl.BlockSpec((tm, tn), lambda i, j: (i, j)),
        out_shape=jax.ShapeDtypeStruct((t, n), BF16),
        compiler_params=_params("parallel", "arbitrary"),
        name="merge",
    )(y_ssd, y_mlp, y_fft, w_ssd, w_mlp, w_fft, gates, gates, gates)


def _conv_kernel(u_ref, w_ref, b_ref, o_ref, *, nx, ctx_len):
    u = u_ref[...].astype(F32)
    rb = u.shape[0]
    width = jnp.where(pl.program_id(0) >= nx, ctx_len, GRID_W)
    pos = lax.broadcasted_iota(jnp.int32, (rb, 1), 0) & (width - 1)
    y = jnp.broadcast_to(b_ref[...], u.shape)
    for k in range(D_CONV):
        off = k - D_CONV // 2
        shifted = u if off == 0 else pltpu.roll(u, (-off) % rb, 0)
        valid = (pos + off >= 0) & (pos + off < width)
        y = y + jnp.where(valid, shifted, 0.0) * w_ref[k:k + 1, :]
    o_ref[...] = _silu(y).astype(o_ref.dtype)


def conv_silu(zx, col0, conv_w, conv_b, seq, ctx_len):
    t = zx.shape[0]
    c = conv_w.shape[1]
    tc = next(v for v in (1024, 512, 256, 128) if c % v == 0 and col0 % v == 0)
    assert ctx_len == ROW_BLOCK and ROW_BLOCK % GRID_W == 0
    assert GRID_W & (GRID_W - 1) == 0 and ctx_len & (ctx_len - 1) == 0
    nx = seq // ROW_BLOCK
    return pl.pallas_call(
        functools.partial(_conv_kernel, nx=nx, ctx_len=ctx_len),
        grid=(t // ROW_BLOCK, c // tc),
        in_specs=[pl.BlockSpec((ROW_BLOCK, tc), lambda i, j: (i, col0 // tc + j)),
                  pl.BlockSpec((D_CONV, tc), lambda i, j: (0, j)),
                  pl.BlockSpec((1, tc), lambda i, j: (0, j))],
        out_specs=pl.BlockSpec((ROW_BLOCK, tc), lambda i, j: (i, j)),
        out_shape=jax.ShapeDtypeStruct((t, c), BF16),
        compiler_params=_params("parallel", "arbitrary"),
        name="conv_silu",
    )(zx, conv_w, conv_b.reshape(1, c))


LOG2E = float(np.log2(np.e))


def _decay_kernel(raw_ref, bias_ref, a_ref, cs_ref, src_ref, wst_ref, etot_ref, *, heads, chunks):
    qi = lax.broadcasted_iota(jnp.int32, (CHUNK, CHUNK), 0)
    si = lax.broadcasted_iota(jnp.int32, (CHUNK, CHUNK), 1)
    lower, upper = (si <= qi).astype(F32), (si >= qi).astype(F32)
    fwd_col = lax.broadcasted_iota(jnp.int32, (CHUNK, LANES), 1) < heads
    for c in range(chunks):
        rows = slice(c * CHUNK, (c + 1) * CHUNK)
        v = raw_ref[rows, :] + bias_ref[...]
        dt = jnp.maximum(v, 0.0) + jnp.log1p(jnp.exp(-jnp.abs(v)))
        a = dt * a_ref[...]
        pre, suf = _dot(lower, a, HI), _dot(upper, a, HI)
        cs = jnp.where(fwd_col, pre, suf)
        tot = jnp.where(fwd_col[0:1, :], pre[CHUNK - 1:CHUNK, :], suf[0:1, :])
        cs_ref[rows, :] = cs * LOG2E
        src_ref[rows, :] = ((cs - jnp.log(dt)) * LOG2E).T
        wst_ref[rows, :] = (jnp.exp(tot - cs) * dt).T
        etot_ref[c * 8:(c + 1) * 8, :] = jnp.broadcast_to(jnp.exp(tot), (8, LANES))


def ssd_decays(dt_raw, dt_bias, a_neg):
    t = dt_raw.shape[0]
    nc = t // CHUNK
    heads = dt_bias.shape[0] // 2
    chunks = _pick(nc, (10, 6, 5, 4, 3, 2, 1))
    pad = lambda vec: jnp.pad(vec.reshape(1, -1), ((0, 0), (0, LANES - vec.shape[0])))
    big = pl.BlockSpec((chunks * CHUNK, LANES), lambda i: (i, 0))
    vec = pl.BlockSpec((1, LANES), lambda i: (0, 0))
    full = jax.ShapeDtypeStruct((t, LANES), F32)
    return pl.pallas_call(
        functools.partial(_decay_kernel, heads=heads, chunks=chunks),
        grid=(nc // chunks,),
        in_specs=[big, vec, vec],
        out_specs=[big, big, big, pl.BlockSpec((chunks * 8, LANES), lambda i: (i, 0))],
        out_shape=[full, full, full, jax.ShapeDtypeStruct((nc * 8, LANES), F32)],
        compiler_params=_params("parallel"),
        name="ssd_decays",
    )(dt_raw, pad(dt_bias), pad(a_neg))


def _ssd_kernel(*refs, direction, heads, d_ssd, d_state, final):
    if final:
        x_ref, cs_ref, src_ref, wst_ref, etot_ref, yf_ref, z_ref, dsk_ref, g_ref, o_ref, state_ref = refs
    else:
        x_ref, cs_ref, src_ref, wst_ref, etot_ref, o_ref, state_ref = refs
    gp = d_ssd // SSD_GROUPS
    r_heads = gp // SSD_HEAD_DIM
    gn = SSD_GROUPS * d_state

    @pl.when(pl.program_id(0) == 0)
    def _():
        state_ref[...] = jnp.zeros_like(state_ref)

    qi = lax.broadcasted_iota(jnp.int32, (CHUNK, CHUNK), 0)
    si = lax.broadcasted_iota(jnp.int32, (CHUNK, CHUNK), 1)
    causal = (si <= qi) if direction == 0 else (si >= qi)
    cs = cs_ref[...]
    src_t = src_ref[...]
    wst_t = wst_ref[...]
    etot = etot_ref[0:1, :]
    lo = lax.broadcasted_iota(jnp.int32, (CHUNK, LANES), 1) < SSD_HEAD_DIM
    lo_row = lo[0:1, :]

    for g in range(SSD_GROUPS):
        h0 = direction * heads + g * r_heads
        bm = x_ref[:, d_ssd + g * d_state:d_ssd + (g + 1) * d_state]
        cm = x_ref[:, d_ssd + gn + g * d_state:d_ssd + gn + (g + 1) * d_state]
        cb = lax.dot_general(cm, bm, (((1,), (1,)), ((), ())), preferred_element_type=F32)
        bt = bm.astype(F32).T
        y_state = _dot(cm, state_ref[:, g * gp:(g + 1) * gp].astype(BF16))
        ys = []
        for k in range(r_heads // 2):
            ps = slice(g * gp + k * LANES, g * gp + (k + 1) * LANES)
            xd = x_ref[:, ps].astype(F32)
            y_acc = st_acc = ecs = None
            for half in range(2):
                h = h0 + 2 * k + half
                csq = jnp.broadcast_to(cs[:, h:h + 1], (CHUNK, CHUNK))
                m = (cb * jnp.exp2(jnp.where(causal, csq - src_t[h:h + 1, :], -1e30))).astype(BF16)
                xh = (jnp.where(lo, xd, 0.0) if half == 0 else jnp.where(lo, 0.0, xd)).astype(BF16)
                btw = (bt * wst_t[h:h + 1, :]).astype(BF16)
                py, pst = _dot(m, xh), _dot(btw, xh)
                e = jnp.exp2(csq)
                if half == 0:
                    y_acc, st_acc, ecs = py, pst, e
                else:
                    y_acc, st_acc, ecs = y_acc + py, st_acc + pst, jnp.where(lo, ecs, e)
            ys.append(y_acc + ecs * y_state[:, k * LANES:(k + 1) * LANES])
            h = h0 + 2 * k
            decay = jnp.where(lo_row, jnp.broadcast_to(etot[:, h:h + 1], (1, LANES)),
                              jnp.broadcast_to(etot[:, h + 1:h + 2], (1, LANES)))
            state_ref[:, ps] = state_ref[:, ps] * decay + st_acc

        gs = slice(g * gp, (g + 1) * gp)
        y = ys[0] if len(ys) == 1 else jnp.concatenate(ys, axis=1)
        if final:
            y = (yf_ref[:, gs] + y + dsk_ref[:, gs] * x_ref[:, gs].astype(F32)) * _silu(z_ref[:, gs].astype(F32))
            y = y * lax.rsqrt(jnp.mean(y * y, axis=-1, keepdims=True) + EPS) * g_ref[:, gs]
        o_ref[:, gs] = y.astype(o_ref.dtype)


def ssd_mixer(xbc, zx, dt_raw, dt_bias, a_neg, d_skip_x, norm_g, seq, d_ssd, d_state):
    t, c_xbc = xbc.shape
    nc = t // CHUNK
    nxc = seq // CHUNK
    ncc = nc - nxc
    heads = dt_bias.shape[0] // 2
    gp = d_ssd // SSD_GROUPS
    assert d_state == LANES and (gp // SSD_HEAD_DIM) % 2 == 0 and gp % LANES == 0 and 2 * heads <= LANES
    fwd = lambda j: jnp.where(j < ncc, nxc + j, j - ncc)
    bwd = lambda j: nc - 1 - j
    vec = lambda n: pl.BlockSpec((1, n), lambda j: (0, 0))
    decays = ssd_decays(dt_raw, dt_bias, a_neg)

    def call(direction, order, final, extra_in, extra_specs, out_dtype):
        per_chunk = pl.BlockSpec((CHUNK, LANES), lambda j: (order(j), 0))
        return pl.pallas_call(
            functools.partial(_ssd_kernel, direction=direction, heads=heads, d_ssd=d_ssd, d_state=d_state,
                              final=final),
            grid=(nc,),
            in_specs=[pl.BlockSpec((CHUNK, c_xbc), lambda j: (order(j), 0)),
                      per_chunk, per_chunk, per_chunk,
                      pl.BlockSpec((8, LANES), lambda j: (order(j), 0))] + extra_specs,
            out_specs=pl.BlockSpec((CHUNK, d_ssd), lambda j: (order(j), 0)),
            out_shape=jax.ShapeDtypeStruct((t, d_ssd), out_dtype),
            scratch_shapes=[pltpu.VMEM((d_state, d_ssd), F32)],
            compiler_params=_params("arbitrary"),
            name="ssd_bwd_out" if final else "ssd_fwd",
        )(xbc, *decays, *extra_in)

    y_f = call(0, fwd, False, [], [], F32)
    blk = pl.BlockSpec((CHUNK, d_ssd), lambda j: (bwd(j), 0))
    return call(1, bwd, True,
                [y_f, zx, d_skip_x.reshape(1, d_ssd), norm_g.reshape(1, d_ssd)],
                [blk, blk, vec(d_ssd), vec(d_ssd)], BF16)


def _sgu_kernel(u_ref, v_ref, g_ref, w_ref, bt_ref, o_ref, *, groups):
    u = _gelu_tanh(u_ref[...].astype(F32))
    v = _gelu_tanh(v_ref[...].astype(F32))
    v = (v * lax.rsqrt(jnp.mean(v * v, axis=-1, keepdims=True) + EPS) * g_ref[...]).astype(BF16)
    for g in range(groups):
        sl = slice(g * GROUP_DIM, (g + 1) * GROUP_DIM)
        vm = _dot(w_ref[g], v[:, sl]) + bt_ref[:, g:g + 1]
        o_ref[:, sl] = (u[:, sl] * vm).astype(o_ref.dtype)


def sgu(uvf, norm_g, w_s, b_s):
    t = uvf.shape[0]
    groups = w_s.shape[0]
    dm = groups * GROUP_DIM
    return pl.pallas_call(
        functools.partial(_sgu_kernel, groups=groups),
        grid=(t // CHUNK,),
        in_specs=[pl.BlockSpec((CHUNK, dm), lambda c: (c, 0)),
                  pl.BlockSpec((CHUNK, dm), lambda c: (c, 1)),
                  pl.BlockSpec((1, dm), lambda c: (0, 0)),
                  pl.BlockSpec((groups, CHUNK, CHUNK), lambda c: (0, 0, 0)),
                  pl.BlockSpec((CHUNK, groups), lambda c: (0, 0))],
        out_specs=pl.BlockSpec((CHUNK, dm), lambda c: (c, 0)),
        out_shape=jax.ShapeDtypeStruct((t, dm), BF16),
        compiler_params=_params("parallel"),
        name="sgu",
    )(uvf, uvf, norm_g.reshape(1, dm), w_s.astype(BF16), b_s.T)


def _channel_dft(f, cs_tab):
    gr, gs = [], []
    for g in range(f.shape[1] // GROUP_DIM):
        p = _dot(f[:, g * GROUP_DIM:(g + 1) * GROUP_DIM], cs_tab)
        gr.append(p[:, :GROUP_DIM])
        gs.append(p[:, GROUP_DIM:])
    cat = lambda v: (v[0] if len(v) == 1 else jnp.concatenate(v, axis=1)).astype(BF16)
    return cat(gr), cat(gs)


def _fft_a_kernel(f_ref, m_ref, cs_ref, z_ref, *, n1):
    gr, gs = _channel_dft(f_ref[...], cs_ref[...])
    m = m_ref[...]
    z_ref[...] = (_dot(m[:, :n1], gr) + _dot(m[:, n1:], gs)).astype(z_ref.dtype)


def _fft_b_kernel(zr_ref, zi_ref, c_ref, s_ref, o_ref, *, scale):
    acc = _dot(c_ref[...], zr_ref[...]) - _dot(s_ref[...], zi_ref[...])
    o_ref[...] = (acc * scale).astype(o_ref.dtype)


def _fft_ctx_kernel(f_ref, c_ref, s_ref, cs_ref, o_ref, *, scale):
    gr, gs = _channel_dft(f_ref[...], cs_ref[...])
    acc = _dot(c_ref[...], gr) - _dot(s_ref[...], gs)
    o_ref[...] = (acc * scale).astype(o_ref.dtype)


def _cos_sin(n):
    ang = 2.0 * np.pi * (np.outer(np.arange(n), np.arange(n)) % n) / n
    return np.cos(ang), np.sin(ang)


def fourier_mix(uvf, col0, d_fft, seq, ctx_len):
    n2 = CHUNK
    n1 = seq // n2
    assert n1 * n2 == seq and n1 % 16 == 0 and d_fft % GROUP_DIM == 0
    cc, sc = _cos_sin(GROUP_DIM)
    cs_tab = jnp.asarray(np.concatenate([cc, sc], axis=1), BF16)
    c1, s1 = _cos_sin(n1)
    ang = 2.0 * np.pi * np.outer(np.arange(n2), np.arange(n1)) / seq
    ct, st = (jnp.asarray(v[:, :, None], F32) for v in (np.cos(ang), np.sin(ang)))
    c1, s1 = jnp.asarray(c1[None], F32), jnp.asarray(s1[None], F32)
    mc = c1 * ct - s1 * st
    ms = s1 * ct + c1 * st
    m_tab = jnp.concatenate([jnp.concatenate([mc, -ms], axis=2),
                             jnp.concatenate([ms, mc], axis=2)], axis=1).astype(BF16)
    c2, s2 = (jnp.asarray(v, BF16) for v in _cos_sin(n2))

    f_t = uvf[:seq, col0:col0 + d_fft].reshape(n1, n2, d_fft).transpose(1, 0, 2)
    z = pl.pallas_call(
        functools.partial(_fft_a_kernel, n1=n1),
        grid=(n2,),
        in_specs=[pl.BlockSpec((None, n1, d_fft), lambda a: (a, 0, 0)),
                  pl.BlockSpec((None, 2 * n1, 2 * n1), lambda a: (a, 0, 0)),
                  pl.BlockSpec((GROUP_DIM, 2 * GROUP_DIM), lambda a: (0, 0))],
        out_specs=pl.BlockSpec((None, 2 * n1, d_fft), lambda a: (a, 0, 0)),
        out_shape=jax.ShapeDtypeStruct((n2, 2 * n1, d_fft), BF16),
        compiler_params=_params("parallel"),
        name="fft_stage_a",
    )(f_t, m_tab, cs_tab)

    width = n1 * d_fft
    tw = _pick(width, (8192, 4096, 2048, 1024, 512, 256, 128))
    nb = width // tw
    z2 = z.reshape(n2, 2 * width)
    tab = lambda: pl.BlockSpec((n2, n2), lambda j: (0, 0))
    y_x = pl.pallas_call(
        functools.partial(_fft_b_kernel, scale=float(1.0 / np.sqrt(seq * GROUP_DIM))),
        grid=(nb,),
        in_specs=[pl.BlockSpec((n2, tw), lambda j: (0, j)),
                  pl.BlockSpec((n2, tw), lambda j: (0, nb + j)),
                  tab(), tab()],
        out_specs=pl.BlockSpec((n2, tw), lambda j: (0, j)),
        out_shape=jax.ShapeDtypeStruct((n2, width), BF16),
        compiler_params=_params("parallel"),
        name="fft_stage_b",
    )(z2, z2, c2, s2).reshape(seq, d_fft)

    assert ctx_len == ROW_BLOCK and col0 % d_fft == 0
    cl, sl = (jnp.asarray(v, BF16) for v in _cos_sin(ctx_len))
    ctab = lambda: pl.BlockSpec((ctx_len, ctx_len), lambda j: (0, 0))
    y_c = pl.pallas_call(
        functools.partial(_fft_ctx_kernel, scale=float(1.0 / np.sqrt(ctx_len * GROUP_DIM))),
        grid=(1,),
        in_specs=[pl.BlockSpec((ctx_len, d_fft), lambda j: (seq // ctx_len, col0 // d_fft)),
                  ctab(), ctab(),
                  pl.BlockSpec((GROUP_DIM, 2 * GROUP_DIM), lambda j: (0, 0))],
        out_specs=pl.BlockSpec((ctx_len, d_fft), lambda j: (0, 0)),
        out_shape=jax.ShapeDtypeStruct((ctx_len, d_fft), BF16),
        compiler_params=_params("arbitrary"),
        name="fft_ctx",
    )(uvf, cl, sl, cs_tab)
    return jnp.concatenate([y_x, y_c], axis=0)


def kernel(x, c, ctx, c_ctx, ada_w1, ada_w2, ada_b, norm_ffn1, f1_w1, f1_w3, f1_w2, norm_mix, w_in, b_in,
           conv_w, conv_b, a_log, dt_bias, d_skip, norm_ssd, sgu_norm, sgu_w, sgu_b,
           w_br_ssd, w_br_mlp, w_br_fft, w_out, norm_ffn2, f2_w1, f2_w3, f2_w2, norm_final):
    batch, seq, d = x.shape
    ctx_len = ctx.shape[1]
    depth = ada_w1.shape[0]
    assert batch == 1 and seq % ROW_BLOCK == 0
    heads = a_log.shape[-1]
    d_ssd = norm_ssd.shape[-1]
    c_xbc = conv_w.shape[-1]
    d_state = (c_xbc - d_ssd) // (2 * SSD_GROUPS)
    d_mlp = sgu_norm.shape[-1]
    d_fft = w_br_fft.shape[1]
    assert heads * SSD_HEAD_DIM == d_ssd and d_mlp == d_fft
    off_xbc = d_ssd
    off_dt = off_xbc + c_xbc
    off_u = off_dt + 2 * heads
    off_gate = off_u + 2 * d_mlp + d_fft

    cv = jnp.zeros((8, d), F32).at[0].set(c[0]).at[1].set(c_ctx)
    mods = ada_mods(cv, ada_w1, ada_w2, ada_b)[:, 0:2, :].reshape(depth, 2, N_MOD, d)
    bf = lambda w: w.astype(BF16)
    f1_w1, f1_w3, f1_w2, f2_w1, f2_w3, f2_w2 = map(bf, (f1_w1, f1_w3, f1_w2, f2_w1, f2_w3, f2_w2))
    w_br_ssd, w_br_mlp, w_br_fft, w_out = map(bf, (w_br_ssd, w_br_mlp, w_br_fft, w_out))
    w_zx, w_dt, w_rest = split_w_in(w_in, off_dt, off_u)
    b_zx, b_rest = b_in[:, :off_dt], b_in[:, off_u:]
    b_dt = jnp.pad(b_in[:, off_dt:off_u], ((0, 0), (0, LANES - 2 * heads)))
    n_uvf = off_gate - off_u

    def gmul(g, scale):
        return g[None, :] * (1.0 + scale)

    xc = (x[0], ctx[0])
    xg, ssq = scale_first(x[0], ctx[0], gmul(norm_ffn1[0], mods[0, :, 1, :]))
    for i in range(depth):
        sh1, sc1, g1, shm, scm, gm, sh2, sc2, g2 = (mods[i, :, k, :] for k in range(N_MOD))

        up = swiglu_up(xg, ssq, f1_w1, f1_w3, shift_proj(sh1, f1_w1, i), shift_proj(sh1, f1_w3, i), i, seq)
        xc, xg, ssq = down_residual(up, f1_w2, i, xc, g1, 0.5, seq, gmul(norm_mix[i], scm))

        zx, dt_raw = proj(xg, ssq, w_zx, b_zx[i] + shift_proj(shm, w_zx, i), i, 0, off_dt, seq, BF16,
                          dt=(w_dt, b_dt[i] + shift_proj(shm, w_dt, i)), name="proj_zx")
        b_eff = b_rest[i] + shift_proj(shm, w_rest, i)
        uvf = proj(xg, ssq, w_rest, b_eff, i, 0, n_uvf, seq, BF16, name="proj_uvf")
        gates = proj(xg, ssq, w_rest, b_eff, i, n_uvf, 3 * d, seq, BF16, sigmoid=True, name="proj_gates")

        xbc = conv_silu(zx, off_xbc, conv_w[i], conv_b[i], seq, ctx_len)
        a_neg = -jnp.exp(a_log[i].astype(F32)).reshape(-1)
        y_ssd = ssd_mixer(xbc, zx, dt_raw, dt_bias[i].reshape(-1), a_neg,
                          jnp.repeat(d_skip[i], SSD_HEAD_DIM), norm_ssd[i], seq, d_ssd, d_state)
        y_mlp = sgu(uvf, sgu_norm[i], sgu_w[i], sgu_b[i])
        y_fft = fourier_mix(uvf, 2 * d_mlp, d_fft, seq, ctx_len)

        merged = merge(y_ssd, y_mlp, y_fft, w_br_ssd, w_br_mlp, w_br_fft, i, gates)
        xc, xg, ssq = down_residual(merged, w_out, i, xc, gm, 1.0, seq, gmul(norm_ffn2[i], sc2))

        up = swiglu_up(xg, ssq, f2_w1, f2_w3, shift_proj(sh2, f2_w1, i), shift_proj(sh2, f2_w3, i), i, seq)
        if i + 1 < depth:
            xc, xg, ssq = down_residual(up, f2_w2, i, xc, g2, 0.5, seq,
                                        gmul(norm_ffn1[i + 1], mods[i + 1, :, 1, :]))
        else:
            xc = down_residual(up, f2_w2, i, xc, g2, 0.5, seq)

    return final_norm(xc, norm_final, seq).reshape(1, seq, d)
```

```python
import functools

import numpy as np
import jax
import jax.numpy as jnp
from jax import lax
from jax.experimental import pallas as pl
from jax.experimental.pallas import tpu as pltpu

F32 = jnp.float32
BF16 = jnp.bfloat16
HI = lax.Precision.HIGHEST

EPS = 1e-6
GRID_W = 64
SSD_GROUPS = 4
SSD_HEAD_DIM = 64
CHUNK = 128
D_CONV = 5
GROUP_DIM = 128
N_MOD = 9
LANES = 128
ROW_BLOCK = 256
VMEM_LIMIT = 58 * 1024 * 1024


def _pick(n, cands):
    for c in cands:
        if n % c == 0:
            return c
    raise ValueError(f"no tile for {n} in {cands}")


def _params(*sem):
    return pltpu.CompilerParams(dimension_semantics=sem, vmem_limit_bytes=VMEM_LIMIT)


def _silu(v):
    return v * jax.nn.sigmoid(v)


def _gelu_tanh(v):
    return v * (0.5 * (1.0 + jnp.tanh(np.sqrt(2.0 / np.pi).astype(np.float32) * (v + 0.044715 * (v * v * v)))))


def _dot(a, b, precision=None):
    return jnp.dot(a, b, preferred_element_type=F32, precision=precision)


def _ada_kernel(cv_ref, w1_ref, w2_ref, b_ref, o_ref):
    t = _dot(_silu(cv_ref[...]), w1_ref[0], HI)
    o_ref[0] = _dot(t, w2_ref[0], HI) + b_ref[0]


def ada_mods(cv, w1, w2, b):
    nl, d, r = w1.shape
    n = w2.shape[-1]
    tn = _pick(n, (4608, 4096, 2304, 2048, 1024, 512, 256, 128))
    return pl.pallas_call(
        _ada_kernel,
        grid=(nl, n // tn),
        in_specs=[pl.BlockSpec((8, d), lambda l, j: (0, 0)),
                  pl.BlockSpec((1, d, r), lambda l, j: (l, 0, 0)),
                  pl.BlockSpec((1, r, tn), lambda l, j: (l, 0, j)),
                  pl.BlockSpec((1, 1, tn), lambda l, j: (l, 0, j))],
        out_specs=pl.BlockSpec((1, 8, tn), lambda l, j: (l, 0, j)),
        out_shape=jax.ShapeDtypeStruct((nl, 8, n), F32),
        compiler_params=_params("arbitrary", "arbitrary"),
        name="ada_mods",
    )(cv, w1, w2, b.reshape(nl, 1, n))


def _lane_partial_sumsq(x):
    acc = None
    for k in range(x.shape[1] // LANES):
        v = x[:, k * LANES:(k + 1) * LANES]
        acc = v * v if acc is None else acc + v * v
    return acc


def _rstd(ssq_ref, d):
    return lax.rsqrt(jnp.sum(ssq_ref[...], axis=-1, keepdims=True) * (1.0 / d) + EPS)


def _scale_first_kernel(x_ref, c_ref, g_ref, xg_ref, ssq_ref, *, nx):
    def emit(src_ref):
        x = src_ref[...]
        xg_ref[...] = (x * g_ref[0]).astype(xg_ref.dtype)
        ssq_ref[...] = _lane_partial_sumsq(x)

    pl.when(pl.program_id(0) < nx)(lambda: emit(x_ref))
    pl.when(pl.program_id(0) >= nx)(lambda: emit(c_ref))


def scale_first(x, ctx, gmul):
    seq, d = x.shape
    t = seq + ctx.shape[0]
    nx = seq // ROW_BLOCK
    return pl.pallas_call(
        functools.partial(_scale_first_kernel, nx=nx),
        grid=(t // ROW_BLOCK,),
        in_specs=[pl.BlockSpec((ROW_BLOCK, d), lambda i: (jnp.minimum(i, nx - 1), 0)),
                  pl.BlockSpec((ROW_BLOCK, d), lambda i: (jnp.maximum(i - nx, 0), 0)),
                  pl.BlockSpec((1, 1, d), lambda i: (jnp.where(i >= nx, 1, 0), 0, 0))],
        out_specs=[pl.BlockSpec((ROW_BLOCK, d), lambda i: (i, 0)),
                   pl.BlockSpec((ROW_BLOCK, LANES), lambda i: (i, 0))],
        out_shape=[jax.ShapeDtypeStruct((t, d), BF16), jax.ShapeDtypeStruct((t, LANES), F32)],
        compiler_params=_params("parallel"),
        name="scale_first",
    )(x, ctx, gmul.reshape(2, 1, d))


def _final_norm_kernel(x_ref, g_ref, o_ref):
    x = x_ref[...]
    o_ref[...] = x * lax.rsqrt(jnp.mean(x * x, axis=-1, keepdims=True) + EPS) * g_ref[...]


def final_norm(xc, g, seq):
    d = xc.shape[1]
    return pl.pallas_call(
        _final_norm_kernel,
        grid=(seq // ROW_BLOCK,),
        in_specs=[pl.BlockSpec((ROW_BLOCK, d), lambda i: (i, 0)),
                  pl.BlockSpec((1, d), lambda i: (0, 0))],
        out_specs=pl.BlockSpec((ROW_BLOCK, d), lambda i: (i, 0)),
        out_shape=jax.ShapeDtypeStruct((seq, d), F32),
        compiler_params=_params("parallel"),
        name="final_norm",
    )(xc, g.reshape(1, d))


def _mm_tiles(t, n, wide=False):
    tm = _pick(t, (1280, 1024, 768, 640, 512, 384, 256))
    tn = _pick(n, ((1024,) if wide else ()) + (512, 384, 256, 128))
    return tm, tn


def _w_spec(k, tn, layer, col0=0):
    return pl.BlockSpec((None, k, tn), lambda i, j: (layer, 0, col0 + j))


def _row_kind(vec, tm, seq):
    rows = pl.program_id(0) * tm + lax.broadcasted_iota(jnp.int32, (tm, 1), 0)
    return jnp.where(rows >= seq, vec[1:2, :], vec[0:1, :])


def _projected_shift(s_ref, w_ref, sw_ref):
    j = pl.program_id(1)

    @pl.when(pl.program_id(0) == 0)
    def _():
        sw_ref[j] = _dot(s_ref[...], w_ref[...])

    return sw_ref[j]


def _shift_rows(shift):
    return jnp.pad(shift, ((0, 14), (0, 0))).astype(BF16)


def _up_kernel(xg_ref, ssq_ref, w1_ref, w3_ref, s_ref, o_ref, sw1_ref, sw3_ref, *, tm, seq):
    xg = xg_ref[...]
    rstd = _rstd(ssq_ref, xg.shape[1])
    a = rstd * _dot(xg, w1_ref[...]) + _row_kind(_projected_shift(s_ref, w1_ref, sw1_ref), tm, seq)
    b = rstd * _dot(xg, w3_ref[...]) + _row_kind(_projected_shift(s_ref, w3_ref, sw3_ref), tm, seq)
    o_ref[...] = (_silu(a) * b).astype(o_ref.dtype)


def swiglu_up(xg, ssq, w1, w3, shift, layer, seq):
    t, d = xg.shape
    n = w1.shape[2]
    tm, tn = _mm_tiles(t, n)
    return pl.pallas_call(
        functools.partial(_up_kernel, tm=tm, seq=seq),
        grid=(t // tm, n // tn),
        in_specs=[pl.BlockSpec((tm, d), lambda i, j: (i, 0)),
                  pl.BlockSpec((tm, LANES), lambda i, j: (i, 0)),
                  _w_spec(d, tn, layer), _w_spec(d, tn, layer),
                  pl.BlockSpec((16, d), lambda i, j: (0, 0))],
        out_specs=pl.BlockSpec((tm, tn), lambda i, j: (i, j)),
        out_shape=jax.ShapeDtypeStruct((t, n), BF16),
        scratch_shapes=[pltpu.VMEM((n // tn, 16, tn), F32)] * 2,
        compiler_params=_params("arbitrary", "arbitrary"),
        name="swiglu_up",
    )(xg, ssq, w1, w3, _shift_rows(shift))


def _down_kernel(*refs, coef, tm, seq, n_latent_last, emit_next):
    split = n_latent_last is not None
    it = iter(refs)
    a_ref, w_ref, x_ref = next(it), next(it), next(it)
    c_ref = next(it) if split else None
    gate_ref = next(it)
    gn_ref = next(it) if emit_next else None
    o_ref = next(it)
    xg_ref, ssq_ref = (next(it), next(it)) if emit_next else (None, None)
    i, j = pl.program_id(0), pl.program_id(1)
    upd = (coef * _row_kind(gate_ref, tm, seq)) * _dot(a_ref[...], w_ref[...])
    gn = _row_kind(gn_ref, tm, seq) if emit_next else None

    def finish(new, rows):
        o_ref[rows, :] = new
        if emit_next:
            xg_ref[rows, :] = (new * gn[rows, :]).astype(xg_ref.dtype)
            part = _lane_partial_sumsq(new)

            @pl.when(j == 0)
            def _():
                ssq_ref[rows, :] = part

            @pl.when(j > 0)
            def _():
                ssq_ref[rows, :] = ssq_ref[rows, :] + part

    if not split:
        finish(x_ref[...] + upd, slice(None))
        return
    last = pl.num_programs(0) - 1

    @pl.when(i < last)
    def _():
        finish(x_ref[...] + upd, slice(None))

    @pl.when(i == last)
    def _():
        lat, cx = slice(0, n_latent_last), slice(n_latent_last, tm)
        finish(x_ref[lat, :] + upd[lat, :], lat)
        finish(c_ref[...] + upd[cx, :], cx)


def down_residual(a, w, layer, res, gate, coef, seq, next_gmul=None):
    t, k = a.shape
    n = w.shape[2]
    tm, tn = _mm_tiles(t, n)
    split = isinstance(res, tuple)
    emit_next = next_gmul is not None
    tile = pl.BlockSpec((tm, tn), lambda i, j: (i, j))
    vec = pl.BlockSpec((2, tn), lambda i, j: (0, j))
    if split:
        ctx_len = res[1].shape[0]
        n_latent_last = tm - ctx_len
        assert res[0].shape[0] == seq and seq + ctx_len == t and 0 < n_latent_last and n_latent_last % 16 == 0
        res_specs = [tile, pl.BlockSpec((ctx_len, tn), lambda i, j: (0, j))]
    else:
        res, n_latent_last = (res,), None
        res_specs = [tile]
    out_specs, out_shape = [tile], [jax.ShapeDtypeStruct((t, n), F32)]
    if emit_next:
        out_specs += [tile, pl.BlockSpec((tm, LANES), lambda i, j: (i, 0))]
        out_shape += [jax.ShapeDtypeStruct((t, n), BF16), jax.ShapeDtypeStruct((t, LANES), F32)]
    out = pl.pallas_call(
        functools.partial(_down_kernel, coef=coef, tm=tm, seq=seq, n_latent_last=n_latent_last,
                          emit_next=emit_next),
        grid=(t // tm, n // tn),
        in_specs=[pl.BlockSpec((tm, k), lambda i, j: (i, 0)), _w_spec(k, tn, layer)] + res_specs
                 + [vec] * (1 + emit_next),
        out_specs=out_specs,
        out_shape=out_shape,
        input_output_aliases={} if split else {2: 0},
        compiler_params=_params("parallel", "arbitrary"),
        name="down_residual",
    )(a, w, *res, gate, *([next_gmul] if emit_next else []))
    return out if emit_next else out[0]


def _proj_kernel(xg_ref, ssq_ref, w_ref, b_ref, s_ref, o_ref, sw_ref, *, tm, seq, sigmoid):
    xg = xg_ref[...]
    shift = _row_kind(_projected_shift(s_ref, w_ref, sw_ref), tm, seq)
    acc = _rstd(ssq_ref, xg.shape[1]) * _dot(xg, w_ref[...]) + (shift + b_ref[...])
    if sigmoid:
        acc = jax.nn.sigmoid(acc)
    o_ref[...] = acc.astype(o_ref.dtype)


def proj(xg, ssq, w, b, shift, layer, col0, n, seq, out_dtype, sigmoid=False, name="proj"):
    t, d = xg.shape
    tm, tn = _mm_tiles(t, n, wide=True)
    while col0 % tn:
        tn //= 2
    assert tn % LANES == 0 and n % tn == 0
    nl, nw = b.shape
    return pl.pallas_call(
        functools.partial(_proj_kernel, tm=tm, seq=seq, sigmoid=sigmoid),
        grid=(t // tm, n // tn),
        in_specs=[pl.BlockSpec((tm, d), lambda i, j: (i, 0)),
                  pl.BlockSpec((tm, LANES), lambda i, j: (i, 0)),
                  _w_spec(d, tn, layer, col0 // tn),
                  pl.BlockSpec((None, 1, tn), lambda i, j: (layer, 0, col0 // tn + j)),
                  pl.BlockSpec((16, d), lambda i, j: (0, 0))],
        out_specs=pl.BlockSpec((tm, tn), lambda i, j: (i, j)),
        out_shape=jax.ShapeDtypeStruct((t, n), out_dtype),
        scratch_shapes=[pltpu.VMEM((n // tn, 16, tn), F32)],
        compiler_params=_params("arbitrary", "arbitrary"),
        name=name,
    )(xg, ssq, w, b.reshape(nl, 1, nw), _shift_rows(shift))


def _merge_kernel(ys_ref, ym_ref, yf_ref, ws_ref, wm_ref, wf_ref, gs_ref, gm_ref, gf_ref, o_ref):
    acc = gs_ref[...].astype(F32) * _dot(ys_ref[...], ws_ref[...])
    acc = acc + gm_ref[...].astype(F32) * _dot(ym_ref[...], wm_ref[...])
    acc = acc + gf_ref[...].astype(F32) * _dot(yf_ref[...], wf_ref[...])
    o_ref[...] = acc.astype(o_ref.dtype)


def merge(y_ssd, y_mlp, y_fft, w_ssd, w_mlp, w_fft, layer, gates):
    t = y_ssd.shape[0]
    n = w_ssd.shape[2]
    tm, tn = _mm_tiles(t, n)
    nb = n // tn
    row = lambda k: pl.BlockSpec((tm, k), lambda i, j: (i, 0))
    col = lambda k: _w_spec(k, tn, layer)
    gate = lambda b: pl.BlockSpec((tm, tn), lambda i, j: (i, b * nb + j))
    return pl.pallas_call(
        _merge_kernel,
        grid=(t // tm, nb),
        in_specs=[row(y_ssd.shape[1]), row(y_mlp.shape[1]), row(y_fft.shape[1]),
                  col(w_ssd.shape[1]), col(w_mlp.shape[1]), col(w_fft.shape[1]),
                  gate(0), gate(1), gate(2)],
        out_specs=pl.BlockSpec((tm, tn), lambda i, j: (i, j)),
        out_shape=jax.ShapeDtypeStruct((t, n), BF16),
        compiler_params=_params("parallel", "arbitrary"),
        name="merge",
    )(y_ssd, y_mlp, y_fft, w_ssd, w_mlp, w_fft, gates, gates, gates)


def _conv_kernel(u_ref, w_ref, b_ref, o_ref, *, nx, ctx_len):
    u = u_ref[...].astype(F32)
    rb = u.shape[0]
    width = jnp.where(pl.program_id(0) >= nx, ctx_len, GRID_W)
    pos = lax.broadcasted_iota(jnp.int32, (rb, 1), 0) & (width - 1)
    y = jnp.broadcast_to(b_ref[...], u.shape)
    for k in range(D_CONV):
        off = k - D_CONV // 2
        shifted = u if off == 0 else pltpu.roll(u, (-off) % rb, 0)
        valid = (pos + off >= 0) & (pos + off < width)
        y = y + jnp.where(valid, shifted, 0.0) * w_ref[k:k + 1, :]
    o_ref[...] = _silu(y).astype(o_ref.dtype)


def conv_silu(zx, col0, conv_w, conv_b, seq, ctx_len):
    t = zx.shape[0]
    c = conv_w.shape[1]
    tc = next(v for v in (1024, 512, 256, 128) if c % v == 0 and col0 % v == 0)
    assert ctx_len == ROW_BLOCK and ROW_BLOCK % GRID_W == 0
    assert GRID_W & (GRID_W - 1) == 0 and ctx_len & (ctx_len - 1) == 0
    nx = seq // ROW_BLOCK
    return pl.pallas_call(
        functools.partial(_conv_kernel, nx=nx, ctx_len=ctx_len),
        grid=(t // ROW_BLOCK, c // tc),
        in_specs=[pl.BlockSpec((ROW_BLOCK, tc), lambda i, j: (i, col0 // tc + j)),
                  pl.BlockSpec((D_CONV, tc), lambda i, j: (0, j)),
                  pl.BlockSpec((1, tc), lambda i, j: (0, j))],
        out_specs=pl.BlockSpec((ROW_BLOCK, tc), lambda i, j: (i, j)),
        out_shape=jax.ShapeDtypeStruct((t, c), BF16),
        compiler_params=_params("parallel", "arbitrary"),
        name="conv_silu",
    )(zx, conv_w, conv_b.reshape(1, c))


LOG2E = float(np.log2(np.e))


def _decay_kernel(raw_ref, bias_ref, a_ref, cs_ref, src_ref, wst_ref, etot_ref, *, heads, chunks):
    qi = lax.broadcasted_iota(jnp.int32, (CHUNK, CHUNK), 0)
    si = lax.broadcasted_iota(jnp.int32, (CHUNK, CHUNK), 1)
    lower, upper = (si <= qi).astype(F32), (si >= qi).astype(F32)
    fwd_col = lax.broadcasted_iota(jnp.int32, (CHUNK, LANES), 1) < heads
    for c in range(chunks):
        rows = slice(c * CHUNK, (c + 1) * CHUNK)
        v = raw_ref[rows, :] + bias_ref[...]
        dt = jnp.maximum(v, 0.0) + jnp.log1p(jnp.exp(-jnp.abs(v)))
        a = dt * a_ref[...]
        pre, suf = _dot(lower, a, HI), _dot(upper, a, HI)
        cs = jnp.where(fwd_col, pre, suf)
        tot = jnp.where(fwd_col[0:1, :], pre[CHUNK - 1:CHUNK, :], suf[0:1, :])
        cs_ref[rows, :] = cs * LOG2E
        src_ref[rows, :] = ((cs - jnp.log(dt)) * LOG2E).T
        wst_ref[rows, :] = (jnp.exp(tot - cs) * dt).T
        etot_ref[c * 8:(c + 1) * 8, :] = jnp.broadcast_to(jnp.exp(tot), (8, LANES))


def ssd_decays(dt_raw, dt_bias, a_neg):
    t = dt_raw.shape[0]
    nc = t // CHUNK
    heads = dt_bias.shape[0] // 2
    chunks = _pick(nc, (10, 6, 5, 4, 3, 2, 1))
    pad = lambda vec: jnp.pad(vec.reshape(1, -1), ((0, 0), (0, LANES - vec.shape[0])))
    big = pl.BlockSpec((chunks * CHUNK, LANES), lambda i: (i, 0))
    vec = pl.BlockSpec((1, LANES), lambda i: (0, 0))
    full = jax.ShapeDtypeStruct((t, LANES), F32)
    return pl.pallas_call(
        functools.partial(_decay_kernel, heads=heads, chunks=chunks),
        grid=(nc // chunks,),
        in_specs=[big, vec, vec],
        out_specs=[big, big, big, pl.BlockSpec((chunks * 8, LANES), lambda i: (i, 0))],
        out_shape=[full, full, full, jax.ShapeDtypeStruct((nc * 8, LANES), F32)],
        compiler_params=_params("parallel"),
        name="ssd_decays",
    )(dt_raw, pad(dt_bias), pad(a_neg))


def _ssd_kernel(*refs, direction, heads, d_ssd, d_state, final):
    if final:
        x_ref, cs_ref, src_ref, wst_ref, etot_ref, yf_ref, z_ref, dsk_ref, g_ref, o_ref, state_ref = refs
    else:
        x_ref, cs_ref, src_ref, wst_ref, etot_ref, o_ref, state_ref = refs
    gp = d_ssd // SSD_GROUPS
    r_heads = gp // SSD_HEAD_DIM
    gn = SSD_GROUPS * d_state

    @pl.when(pl.program_id(0) == 0)
    def _():
        state_ref[...] = jnp.zeros_like(state_ref)

    qi = lax.broadcasted_iota(jnp.int32, (CHUNK, CHUNK), 0)
    si = lax.broadcasted_iota(jnp.int32, (CHUNK, CHUNK), 1)
    causal = (si <= qi) if direction == 0 else (si >= qi)
    cs = cs_ref[...]
    src_t = src_ref[...]
    wst_t = wst_ref[...]
    etot = etot_ref[0:1, :]
    lo = lax.broadcasted_iota(jnp.int32, (CHUNK, LANES), 1) < SSD_HEAD_DIM
    lo_row = lo[0:1, :]

    for g in range(SSD_GROUPS):
        h0 = direction * heads + g * r_heads
        bm = x_ref[:, d_ssd + g * d_state:d_ssd + (g + 1) * d_state]
        cm = x_ref[:, d_ssd + gn + g * d_state:d_ssd + gn + (g + 1) * d_state]
        cb = lax.dot_general(cm, bm, (((1,), (1,)), ((), ())), preferred_element_type=F32)
        bt = bm.astype(F32).T
        y_state = _dot(cm, state_ref[:, g * gp:(g + 1) * gp].astype(BF16))
        ys = []
        for k in range(r_heads // 2):
            ps = slice(g * gp + k * LANES, g * gp + (k + 1) * LANES)
            xd = x_ref[:, ps].astype(F32)
            y_acc = st_acc = ecs = None
            for half in range(2):
                h = h0 + 2 * k + half
                csq = jnp.broadcast_to(cs[:, h:h + 1], (CHUNK, CHUNK))
                m = (cb * jnp.exp2(jnp.where(causal, csq - src_t[h:h + 1, :], -1e30))).astype(BF16)
                xh = (jnp.where(lo, xd, 0.0) if half == 0 else jnp.where(lo, 0.0, xd)).astype(BF16)
                btw = (bt * wst_t[h:h + 1, :]).astype(BF16)
                py, pst = _dot(m, xh), _dot(btw, xh)
                e = jnp.exp2(csq)
                if half == 0:
                    y_acc, st_acc, ecs = py, pst, e
                else:
                    y_acc, st_acc, ecs = y_acc + py, st_acc + pst, jnp.where(lo, ecs, e)
            ys.append(y_acc + ecs * y_state[:, k * LANES:(k + 1) * LANES])
            h = h0 + 2 * k
            decay = jnp.where(lo_row, jnp.broadcast_to(etot[:, h:h + 1], (1, LANES)),
                              jnp.broadcast_to(etot[:, h + 1:h + 2], (1, LANES)))
            state_ref[:, ps] = state_ref[:, ps] * decay + st_acc

        gs = slice(g * gp, (g + 1) * gp)
        y = ys[0] if len(ys) == 1 else jnp.concatenate(ys, axis=1)
        if final:
            y = (yf_ref[:, gs] + y + dsk_ref[:, gs] * x_ref[:, gs].astype(F32)) * _silu(z_ref[:, gs].astype(F32))
            y = y * lax.rsqrt(jnp.mean(y * y, axis=-1, keepdims=True) + EPS) * g_ref[:, gs]
        o_ref[:, gs] = y.astype(o_ref.dtype)


def ssd_mixer(xbc, zx, dt_raw, dt_bias, a_neg, d_skip_x, norm_g, seq, d_ssd, d_state):
    t, c_xbc = xbc.shape
    nc = t // CHUNK
    nxc = seq // CHUNK
    ncc = nc - nxc
    heads = dt_bias.shape[0] // 2
    gp = d_ssd // SSD_GROUPS
    assert d_state == LANES and (gp // SSD_HEAD_DIM) % 2 == 0 and gp % LANES == 0 and 2 * heads <= LANES
    fwd = lambda j: jnp.where(j < ncc, nxc + j, j - ncc)
    bwd = lambda j: nc - 1 - j
    vec = lambda n: pl.BlockSpec((1, n), lambda j: (0, 0))
    decays = ssd_decays(dt_raw, dt_bias, a_neg)

    def call(direction, order, final, extra_in, extra_specs, out_dtype):
        per_chunk = pl.BlockSpec((CHUNK, LANES), lambda j: (order(j), 0))
        return pl.pallas_call(
            functools.partial(_ssd_kernel, direction=direction, heads=heads, d_ssd=d_ssd, d_state=d_state,
                              final=final),
            grid=(nc,),
            in_specs=[pl.BlockSpec((CHUNK, c_xbc), lambda j: (order(j), 0)),
                      per_chunk, per_chunk, per_chunk,
                      pl.BlockSpec((8, LANES), lambda j: (order(j), 0))] + extra_specs,
            out_specs=pl.BlockSpec((CHUNK, d_ssd), lambda j: (order(j), 0)),
            out_shape=jax.ShapeDtypeStruct((t, d_ssd), out_dtype),
            scratch_shapes=[pltpu.VMEM((d_state, d_ssd), F32)],
            compiler_params=_params("arbitrary"),
            name="ssd_bwd_out" if final else "ssd_fwd",
        )(xbc, *decays, *extra_in)

    y_f = call(0, fwd, False, [], [], F32)
    blk = pl.BlockSpec((CHUNK, d_ssd), lambda j: (bwd(j), 0))
    return call(1, bwd, True,
                [y_f, zx, d_skip_x.reshape(1, d_ssd), norm_g.reshape(1, d_ssd)],
                [blk, blk, vec(d_ssd), vec(d_ssd)], BF16)


def _sgu_kernel(u_ref, v_ref, g_ref, w_ref, bt_ref, o_ref, *, groups):
    for c in range(u_ref.shape[0] // CHUNK):
        rows = slice(c * CHUNK, (c + 1) * CHUNK)
        u = _gelu_tanh(u_ref[rows, :].astype(F32))
        v = _gelu_tanh(v_ref[rows, :].astype(F32))
        v = (v * lax.rsqrt(jnp.mean(v * v, axis=-1, keepdims=True) + EPS) * g_ref[...]).astype(BF16)
        for g in range(groups):
            sl = slice(g * GROUP_DIM, (g + 1) * GROUP_DIM)
            vm = _dot(w_ref[g], v[:, sl]) + bt_ref[:, g:g + 1]
            o_ref[rows, sl] = (u[:, sl] * vm).astype(o_ref.dtype)


def sgu(uvf, norm_g, w_s, b_s):
    t = uvf.shape[0]
    groups = w_s.shape[0]
    dm = groups * GROUP_DIM
    rb = 2 * CHUNK
    assert t % rb == 0
    return pl.pallas_call(
        functools.partial(_sgu_kernel, groups=groups),
        grid=(t // rb,),
        in_specs=[pl.BlockSpec((rb, dm), lambda c: (c, 0)),
                  pl.BlockSpec((rb, dm), lambda c: (c, 1)),
                  pl.BlockSpec((1, dm), lambda c: (0, 0)),
                  pl.BlockSpec((groups, CHUNK, CHUNK), lambda c: (0, 0, 0)),
                  pl.BlockSpec((CHUNK, groups), lambda c: (0, 0))],
        out_specs=pl.BlockSpec((rb, dm), lambda c: (c, 0)),
        out_shape=jax.ShapeDtypeStruct((t, dm), BF16),
        compiler_params=_params("parallel"),
        name="sgu",
    )(uvf, uvf, norm_g.reshape(1, dm), w_s.astype(BF16), b_s.T)


def _channel_dft(f, cs_tab):
    gr, gs = [], []
    for g in range(f.shape[1] // GROUP_DIM):
        p = _dot(f[:, g * GROUP_DIM:(g + 1) * GROUP_DIM], cs_tab)
        gr.append(p[:, :GROUP_DIM])
        gs.append(p[:, GROUP_DIM:])
    cat = lambda v: (v[0] if len(v) == 1 else jnp.concatenate(v, axis=1)).astype(BF16)
    return cat(gr), cat(gs)


def _fft_a_kernel(f_ref, m_ref, cs_ref, z_ref, *, n1):
    for a in range(f_ref.shape[0]):
        gr, gs = _channel_dft(f_ref[a], cs_ref[...])
        m = m_ref[a]
        z_ref[a] = (_dot(m[:, :n1], gr) + _dot(m[:, n1:], gs)).astype(z_ref.dtype)


def _fft_b_kernel(zr_ref, zi_ref, c_ref, s_ref, o_ref, *, scale):
    acc = _dot(c_ref[...], zr_ref[...]) - _dot(s_ref[...], zi_ref[...])
    o_ref[...] = (acc * scale).astype(o_ref.dtype)


def _fft_ctx_kernel(f_ref, c_ref, s_ref, cs_ref, o_ref, *, scale):
    gr, gs = _channel_dft(f_ref[...], cs_ref[...])
    acc = _dot(c_ref[...], gr) - _dot(s_ref[...], gs)
    o_ref[...] = (acc * scale).astype(o_ref.dtype)


def _cos_sin(n):
    ang = 2.0 * np.pi * (np.outer(np.arange(n), np.arange(n)) % n) / n
    return np.cos(ang), np.sin(ang)


def fourier_mix(uvf, col0, d_fft, seq, ctx_len):
    n2 = CHUNK
    n1 = seq // n2
    assert n1 * n2 == seq and n1 % 16 == 0 and d_fft % GROUP_DIM == 0
    cc, sc = _cos_sin(GROUP_DIM)
    cs_tab = jnp.asarray(np.concatenate([cc, sc], axis=1), BF16)
    c1, s1 = _cos_sin(n1)
    ang = 2.0 * np.pi * np.outer(np.arange(n2), np.arange(n1)) / seq
    ct, st = (jnp.asarray(v[:, :, None], F32) for v in (np.cos(ang), np.sin(ang)))
    c1, s1 = jnp.asarray(c1[None], F32), jnp.asarray(s1[None], F32)
    mc = c1 * ct - s1 * st
    ms = s1 * ct + c1 * st
    m_tab = jnp.concatenate([jnp.concatenate([mc, -ms], axis=2),
                             jnp.concatenate([ms, mc], axis=2)], axis=1).astype(BF16)
    c2, s2 = (jnp.asarray(v, BF16) for v in _cos_sin(n2))

    f_t = uvf[:seq, col0:col0 + d_fft].reshape(n1, n2, d_fft).transpose(1, 0, 2)
    ta = 4
    z = pl.pallas_call(
        functools.partial(_fft_a_kernel, n1=n1),
        grid=(n2 // ta,),
        in_specs=[pl.BlockSpec((ta, n1, d_fft), lambda a: (a, 0, 0)),
                  pl.BlockSpec((ta, 2 * n1, 2 * n1), lambda a: (a, 0, 0)),
                  pl.BlockSpec((GROUP_DIM, 2 * GROUP_DIM), lambda a: (0, 0))],
        out_specs=pl.BlockSpec((ta, 2 * n1, d_fft), lambda a: (a, 0, 0)),
        out_shape=jax.ShapeDtypeStruct((n2, 2 * n1, d_fft), BF16),
        compiler_params=_params("parallel"),
        name="fft_stage_a",
    )(f_t, m_tab, cs_tab)

    width = n1 * d_fft
    tw = _pick(width, (8192, 4096, 2048, 1024, 512, 256, 128))
    nb = width // tw
    z2 = z.reshape(n2, 2 * width)
    tab = lambda: pl.BlockSpec((n2, n2), lambda j: (0, 0))
    y_x = pl.pallas_call(
        functools.partial(_fft_b_kernel, scale=float(1.0 / np.sqrt(seq * GROUP_DIM))),
        grid=(nb,),
        in_specs=[pl.BlockSpec((n2, tw), lambda j: (0, j)),
                  pl.BlockSpec((n2, tw), lambda j: (0, nb + j)),
                  tab(), tab()],
        out_specs=pl.BlockSpec((n2, tw), lambda j: (0, j)),
        out_shape=jax.ShapeDtypeStruct((n2, width), BF16),
        compiler_params=_params("parallel"),
        name="fft_stage_b",
    )(z2, z2, c2, s2).reshape(seq, d_fft)

    assert ctx_len == ROW_BLOCK and col0 % d_fft == 0
    cl, sl = (jnp.asarray(v, BF16) for v in _cos_sin(ctx_len))
    ctab = lambda: pl.BlockSpec((ctx_len, ctx_len), lambda j: (0, 0))
    y_c = pl.pallas_call(
        functools.partial(_fft_ctx_kernel, scale=float(1.0 / np.sqrt(ctx_len * GROUP_DIM))),
        grid=(1,),
        in_specs=[pl.BlockSpec((ctx_len, d_fft), lambda j: (seq // ctx_len, col0 // d_fft)),
                  ctab(), ctab(),
                  pl.BlockSpec((GROUP_DIM, 2 * GROUP_DIM), lambda j: (0, 0))],
        out_specs=pl.BlockSpec((ctx_len, d_fft), lambda j: (0, 0)),
        out_shape=jax.ShapeDtypeStruct((ctx_len, d_fft), BF16),
        compiler_params=_params("arbitrary"),
        name="fft_ctx",
    )(uvf, cl, sl, cs_tab)
    return jnp.concatenate([y_x, y_c], axis=0)


def kernel(x, c, ctx, c_ctx, ada_w1, ada_w2, ada_b, norm_ffn1, f1_w1, f1_w3, f1_w2, norm_mix, w_in, b_in,
           conv_w, conv_b, a_log, dt_bias, d_skip, norm_ssd, sgu_norm, sgu_w, sgu_b,
           w_br_ssd, w_br_mlp, w_br_fft, w_out, norm_ffn2, f2_w1, f2_w3, f2_w2, norm_final):
    batch, seq, d = x.shape
    ctx_len = ctx.shape[1]
    depth = ada_w1.shape[0]
    assert batch == 1 and seq % ROW_BLOCK == 0
    heads = a_log.shape[-1]
    d_ssd = norm_ssd.shape[-1]
    c_xbc = conv_w.shape[-1]
    d_state = (c_xbc - d_ssd) // (2 * SSD_GROUPS)
    d_mlp = sgu_norm.shape[-1]
    d_fft = w_br_fft.shape[1]
    assert heads * SSD_HEAD_DIM == d_ssd and d_mlp == d_fft
    off_xbc = d_ssd
    off_dt = off_xbc + c_xbc
    off_u = off_dt + 2 * heads
    off_gate = off_u + 2 * d_mlp + d_fft

    cv = jnp.zeros((8, d), F32).at[0].set(c[0]).at[1].set(c_ctx)
    mods = ada_mods(cv, ada_w1, ada_w2, ada_b)[:, 0:2, :].reshape(depth, 2, N_MOD, d)
    bf = lambda w: w.astype(BF16)
    f1_w1, f1_w3, f1_w2, f2_w1, f2_w3, f2_w2 = map(bf, (f1_w1, f1_w3, f1_w2, f2_w1, f2_w3, f2_w2))
    w_br_ssd, w_br_mlp, w_br_fft, w_out = map(bf, (w_br_ssd, w_br_mlp, w_br_fft, w_out))
    dt_pad = LANES - 2 * heads
    w_zx, b_zx = bf(w_in[:, :, :off_dt]), b_in[:, :off_dt]
    w_dt = bf(jnp.pad(w_in[:, :, off_dt:off_u], ((0, 0), (0, 0), (0, dt_pad))))
    b_dt = jnp.pad(b_in[:, off_dt:off_u], ((0, 0), (0, dt_pad)))
    w_rest, b_rest = bf(w_in[:, :, off_u:]), b_in[:, off_u:]
    n_uvf = off_gate - off_u

    def gmul(g, scale):
        return g[None, :] * (1.0 + scale)

    xc = (x[0], ctx[0])
    xg, ssq = scale_first(x[0], ctx[0], gmul(norm_ffn1[0], mods[0, :, 1, :]))
    for i in range(depth):
        sh1, sc1, g1, shm, scm, gm, sh2, sc2, g2 = (mods[i, :, k, :] for k in range(N_MOD))

        up = swiglu_up(xg, ssq, f1_w1, f1_w3, sh1, i, seq)
        xc, xg, ssq = down_residual(up, f1_w2, i, xc, g1, 0.5, seq, gmul(norm_mix[i], scm))

        zx = proj(xg, ssq, w_zx, b_zx, shm, i, 0, off_dt, seq, BF16, name="proj_zx")
        dt_raw = proj(xg, ssq, w_dt, b_dt, shm, i, 0, LANES, seq, F32, name="proj_dt")
        uvf = proj(xg, ssq, w_rest, b_rest, shm, i, 0, n_uvf, seq, BF16, name="proj_uvf")
        gates = proj(xg, ssq, w_rest, b_rest, shm, i, n_uvf, 3 * d, seq, BF16, sigmoid=True, name="proj_gates")

        xbc = conv_silu(zx, off_xbc, conv_w[i], conv_b[i], seq, ctx_len)
        a_neg = -jnp.exp(a_log[i].astype(F32)).reshape(-1)
        y_ssd = ssd_mixer(xbc, zx, dt_raw, dt_bias[i].reshape(-1), a_neg,
                          jnp.repeat(d_skip[i], SSD_HEAD_DIM), norm_ssd[i], seq, d_ssd, d_state)
        y_mlp = sgu(uvf, sgu_norm[i], sgu_w[i], sgu_b[i])
        y_fft = fourier_mix(uvf, 2 * d_mlp, d_fft, seq, ctx_len)

        merged = merge(y_ssd, y_mlp, y_fft, w_br_ssd, w_br_mlp, w_br_fft, i, gates)
        xc, xg, ssq = down_residual(merged, w_out, i, xc, gm, 1.0, seq, gmul(norm_ffn2[i], sc2))

        up = swiglu_up(xg, ssq, f2_w1, f2_w3, sh2, i, seq)
        if i + 1 < depth:
            xc, xg, ssq = down_residual(up, f2_w2, i, xc, g2, 0.5, seq,
                                        gmul(norm_ffn1[i + 1], mods[i + 1, :, 1, :]))
        else:
            xc = down_residual(up, f2_w2, i, xc, g2, 0.5, seq)

    return final_norm(xc, norm_final, seq).reshape(1, seq, d)
```

```python
import functools

import numpy as np
import jax
import jax.numpy as jnp
from jax import lax
from jax.experimental import pallas as pl
from jax.experimental.pallas import tpu as pltpu

F32 = jnp.float32
BF16 = jnp.bfloat16
HI = lax.Precision.HIGHEST

EPS = 1e-6
GRID_W = 64
SSD_GROUPS = 4
SSD_HEAD_DIM = 64
CHUNK = 128
D_CONV = 5
GROUP_DIM = 128
N_MOD = 9
LANES = 128
ROW_BLOCK = 256
VMEM_LIMIT = 56 * 1024 * 1024


def _pick(n, cands):
    for c in cands:
        if n % c == 0:
            return c
    raise ValueError(f"no tile for {n} in {cands}")


def _params(*sem):
    return pltpu.CompilerParams(dimension_semantics=sem, vmem_limit_bytes=VMEM_LIMIT)


def _silu(v):
    return v * jax.nn.sigmoid(v)


def _gelu_tanh(v):
    return v * (0.5 * (1.0 + jnp.tanh(np.sqrt(2.0 / np.pi).astype(np.float32) * (v + 0.044715 * (v * v * v)))))


def _dot(a, b, precision=None):
    return jnp.dot(a, b, preferred_element_type=F32, precision=precision)


def _ada_kernel(cv_ref, w1_ref, w2_ref, b_ref, o_ref):
    t = _dot(_silu(cv_ref[...]), w1_ref[0], HI)
    o_ref[0] = _dot(t, w2_ref[0], HI) + b_ref[0]


def ada_mods(cv, w1, w2, b):
    nl, d, r = w1.shape
    n = w2.shape[-1]
    tn = _pick(n, (4608, 4096, 2304, 2048, 1024, 512, 256, 128))
    return pl.pallas_call(
        _ada_kernel,
        grid=(nl, n // tn),
        in_specs=[pl.BlockSpec((8, d), lambda l, j: (0, 0)),
                  pl.BlockSpec((1, d, r), lambda l, j: (l, 0, 0)),
                  pl.BlockSpec((1, r, tn), lambda l, j: (l, 0, j)),
                  pl.BlockSpec((1, 1, tn), lambda l, j: (l, 0, j))],
        out_specs=pl.BlockSpec((1, 8, tn), lambda l, j: (l, 0, j)),
        out_shape=jax.ShapeDtypeStruct((nl, 8, n), F32),
        compiler_params=_params("arbitrary", "arbitrary"),
        name="ada_mods",
    )(cv, w1, w2, b.reshape(nl, 1, n))


def _norm_mod_kernel(*refs, nx, split):
    if split:
        x_ref, c_ref, g_ref, sh_ref, sc_ref, o_ref = refs
    else:
        x_ref, g_ref, sh_ref, sc_ref, o_ref = refs

    def emit(src_ref):
        x = src_ref[...]
        y = x * lax.rsqrt(jnp.mean(x * x, axis=-1, keepdims=True) + EPS) * g_ref[...]
        o_ref[...] = (y * (1.0 + sc_ref[0]) + sh_ref[0]).astype(o_ref.dtype)

    if split:
        pl.when(pl.program_id(0) < nx)(lambda: emit(x_ref))
        pl.when(pl.program_id(0) >= nx)(lambda: emit(c_ref))
    else:
        emit(x_ref)


def norm_mod(xs, g, shift, scale, seq):
    split = isinstance(xs, tuple)
    d = g.shape[0]
    nx = seq // ROW_BLOCK
    sel = lambda i: (jnp.where(i >= nx, 1, 0), 0, 0)
    if split:
        t = seq + xs[1].shape[0]
        src_specs = [pl.BlockSpec((ROW_BLOCK, d), lambda i: (jnp.minimum(i, nx - 1), 0)),
                     pl.BlockSpec((ROW_BLOCK, d), lambda i: (jnp.maximum(i - nx, 0), 0))]
    else:
        xs = (xs,)
        t = xs[0].shape[0]
        src_specs = [pl.BlockSpec((ROW_BLOCK, d), lambda i: (i, 0))]
    return pl.pallas_call(
        functools.partial(_norm_mod_kernel, nx=nx, split=split),
        grid=(t // ROW_BLOCK,),
        in_specs=src_specs + [pl.BlockSpec((1, d), lambda i: (0, 0)),
                              pl.BlockSpec((1, 1, d), sel),
                              pl.BlockSpec((1, 1, d), sel)],
        out_specs=pl.BlockSpec((ROW_BLOCK, d), lambda i: (i, 0)),
        out_shape=jax.ShapeDtypeStruct((t, d), BF16),
        compiler_params=_params("parallel"),
        name="norm_mod",
    )(*xs, g.reshape(1, d), shift.reshape(2, 1, d), scale.reshape(2, 1, d))


def _final_norm_kernel(x_ref, g_ref, o_ref):
    x = x_ref[...]
    o_ref[...] = x * lax.rsqrt(jnp.mean(x * x, axis=-1, keepdims=True) + EPS) * g_ref[...]


def final_norm(xc, g, seq):
    d = xc.shape[1]
    return pl.pallas_call(
        _final_norm_kernel,
        grid=(seq // ROW_BLOCK,),
        in_specs=[pl.BlockSpec((ROW_BLOCK, d), lambda i: (i, 0)),
                  pl.BlockSpec((1, d), lambda i: (0, 0))],
        out_specs=pl.BlockSpec((ROW_BLOCK, d), lambda i: (i, 0)),
        out_shape=jax.ShapeDtypeStruct((seq, d), F32),
        compiler_params=_params("parallel"),
        name="final_norm",
    )(xc, g.reshape(1, d))


def _mm_tiles(t, n, wide=False):
    tm = _pick(t, (1280, 1024, 768, 640, 512, 384, 256))
    tn = _pick(n, ((1024,) if wide else ()) + (512, 384, 256, 128))
    return tm, tn


def _w_spec(k, tn, layer, col0=0):
    return pl.BlockSpec((None, k, tn), lambda i, j: (layer, 0, col0 + j))


def _up_kernel(h_ref, w1_ref, w3_ref, o_ref):
    h = h_ref[...]
    a = _dot(h, w1_ref[...])
    o_ref[...] = (_silu(a) * _dot(h, w3_ref[...])).astype(o_ref.dtype)


def swiglu_up(h, w1, w3, layer):
    t, d = h.shape
    n = w1.shape[2]
    tm, tn = _mm_tiles(t, n)
    return pl.pallas_call(
        _up_kernel,
        grid=(t // tm, n // tn),
        in_specs=[pl.BlockSpec((tm, d), lambda i, j: (i, 0)),
                  _w_spec(d, tn, layer), _w_spec(d, tn, layer)],
        out_specs=pl.BlockSpec((tm, tn), lambda i, j: (i, j)),
        out_shape=jax.ShapeDtypeStruct((t, n), BF16),
        compiler_params=_params("parallel", "arbitrary"),
        name="swiglu_up",
    )(h, w1, w3)


def _down_kernel(*refs, coef, tm, seq, n_latent_last):
    split = n_latent_last is not None
    if split:
        a_ref, w_ref, x_ref, c_ref, gate_ref, o_ref = refs
    else:
        a_ref, w_ref, x_ref, gate_ref, o_ref = refs
    i = pl.program_id(0)
    rows = i * tm + lax.broadcasted_iota(jnp.int32, (tm, 1), 0)
    gate = jnp.where(rows >= seq, gate_ref[1:2, :], gate_ref[0:1, :])
    upd = (coef * gate) * _dot(a_ref[...], w_ref[...])
    if not split:
        o_ref[...] = x_ref[...] + upd
        return
    last = pl.num_programs(0) - 1

    @pl.when(i < last)
    def _():
        o_ref[...] = x_ref[...] + upd

    @pl.when(i == last)
    def _():
        o_ref[:n_latent_last, :] = x_ref[:n_latent_last, :] + upd[:n_latent_last, :]
        o_ref[n_latent_last:, :] = c_ref[...] + upd[n_latent_last:, :]


def down_residual(a, w, layer, res, gate, coef, seq):
    t, k = a.shape
    n = w.shape[2]
    tm, tn = _mm_tiles(t, n)
    split = isinstance(res, tuple)
    res_spec = pl.BlockSpec((tm, tn), lambda i, j: (i, j))
    if split:
        ctx_len = res[1].shape[0]
        n_latent_last = tm - ctx_len
        assert res[0].shape[0] == seq and seq + ctx_len == t and 0 < n_latent_last and n_latent_last % 8 == 0
        res_specs = [res_spec, pl.BlockSpec((ctx_len, tn), lambda i, j: (0, j))]
    else:
        res, n_latent_last = (res,), None
        res_specs = [res_spec]
    return pl.pallas_call(
        functools.partial(_down_kernel, coef=coef, tm=tm, seq=seq, n_latent_last=n_latent_last),
        grid=(t // tm, n // tn),
        in_specs=[pl.BlockSpec((tm, k), lambda i, j: (i, 0)), _w_spec(k, tn, layer)] + res_specs
                 + [pl.BlockSpec((2, tn), lambda i, j: (0, j))],
        out_specs=pl.BlockSpec((tm, tn), lambda i, j: (i, j)),
        out_shape=jax.ShapeDtypeStruct((t, n), F32),
        input_output_aliases={} if split else {2: 0},
        compiler_params=_params("parallel", "arbitrary"),
        name="down_residual",
    )(a, w, *res, gate)


def _proj_kernel(h_ref, w_ref, b_ref, o_ref, *, sigmoid):
    acc = _dot(h_ref[...], w_ref[...]) + b_ref[...]
    if sigmoid:
        acc = jax.nn.sigmoid(acc)
    o_ref[...] = acc.astype(o_ref.dtype)


def proj(h, w, b, layer, col0, n, out_dtype, sigmoid=False, name="proj"):
    t, d = h.shape
    tm, tn = _mm_tiles(t, n, wide=True)
    while col0 % tn:
        tn //= 2
    assert tn % LANES == 0 and n % tn == 0
    nl, nw = b.shape
    return pl.pallas_call(
        functools.partial(_proj_kernel, sigmoid=sigmoid),
        grid=(t // tm, n // tn),
        in_specs=[pl.BlockSpec((tm, d), lambda i, j: (i, 0)),
                  _w_spec(d, tn, layer, col0 // tn),
                  pl.BlockSpec((None, 1, tn), lambda i, j: (layer, 0, col0 // tn + j))],
        out_specs=pl.BlockSpec((tm, tn), lambda i, j: (i, j)),
        out_shape=jax.ShapeDtypeStruct((t, n), out_dtype),
        compiler_params=_params("parallel", "arbitrary"),
        name=name,
    )(h, w, b.reshape(nl, 1, nw))


def _merge_kernel(ys_ref, ym_ref, yf_ref, ws_ref, wm_ref, wf_ref, gs_ref, gm_ref, gf_ref, o_ref):
    acc = gs_ref[...].astype(F32) * _dot(ys_ref[...], ws_ref[...])
    acc = acc + gm_ref[...].astype(F32) * _dot(ym_ref[...], wm_ref[...])
    acc = acc + gf_ref[...].astype(F32) * _dot(yf_ref[...], wf_ref[...])
    o_ref[...] = acc.astype(o_ref.dtype)


def merge(y_ssd, y_mlp, y_fft, w_ssd, w_mlp, w_fft, layer, gates):
    t = y_ssd.shape[0]
    n = w_ssd.shape[2]
    tm, tn = _mm_tiles(t, n)
    nb = n // tn
    row = lambda k: pl.BlockSpec((tm, k), lambda i, j: (i, 0))
    col = lambda k: _w_spec(k, tn, layer)
    gate = lambda b: pl.BlockSpec((tm, tn), lambda i, j: (i, b * nb + j))
    return pl.pallas_call(
        _merge_kernel,
        grid=(t // tm, nb),
        in_specs=[row(y_ssd.shape[1]), row(y_mlp.shape[1]), row(y_fft.shape[1]),
                  col(w_ssd.shape[1]), col(w_mlp.shape[1]), col(w_fft.shape[1]),
                  gate(0), gate(1), gate(2)],
        out_specs=pl.BlockSpec((tm, tn), lambda i, j: (i, j)),
        out_shape=jax.ShapeDtypeStruct((t, n), BF16),
        compiler_params=_params("parallel", "arbitrary"),
        name="merge",
    )(y_ssd, y_mlp, y_fft, w_ssd, w_mlp, w_fft, gates, gates, gates)


def _conv_kernel(u_ref, w_ref, b_ref, o_ref, *, nx, ctx_len):
    u = u_ref[...].astype(F32)
    rb = u.shape[0]
    width = jnp.where(pl.program_id(0) >= nx, ctx_len, GRID_W)
    pos = lax.broadcasted_iota(jnp.int32, (rb, 1), 0) & (width - 1)
    y = jnp.broadcast_to(b_ref[...], u.shape)
    for k in range(D_CONV):
        off = k - D_CONV // 2
        shifted = u if off == 0 else pltpu.roll(u, (-off) % rb, 0)
        valid = (pos + off >= 0) & (pos + off < width)
        y = y + jnp.where(valid, shifted, 0.0) * w_ref[k:k + 1, :]
    o_ref[...] = _silu(y).astype(o_ref.dtype)


def conv_silu(zx, col0, conv_w, conv_b, seq, ctx_len):
    t = zx.shape[0]
    c = conv_w.shape[1]
    tc = next(v for v in (1024, 512, 256, 128) if c % v == 0 and col0 % v == 0)
    assert ctx_len == ROW_BLOCK and ROW_BLOCK % GRID_W == 0
    assert GRID_W & (GRID_W - 1) == 0 and ctx_len & (ctx_len - 1) == 0
    nx = seq // ROW_BLOCK
    return pl.pallas_call(
        functools.partial(_conv_kernel, nx=nx, ctx_len=ctx_len),
        grid=(t // ROW_BLOCK, c // tc),
        in_specs=[pl.BlockSpec((ROW_BLOCK, tc), lambda i, j: (i, col0 // tc + j)),
                  pl.BlockSpec((D_CONV, tc), lambda i, j: (0, j)),
                  pl.BlockSpec((1, tc), lambda i, j: (0, j))],
        out_specs=pl.BlockSpec((ROW_BLOCK, tc), lambda i, j: (i, j)),
        out_shape=jax.ShapeDtypeStruct((t, c), BF16),
        compiler_params=_params("parallel", "arbitrary"),
        name="conv_silu",
    )(zx, conv_w, conv_b.reshape(1, c))


LOG2E = float(np.log2(np.e))


def _decay_kernel(raw_ref, bias_ref, a_ref, cs_ref, src_ref, wst_ref, etot_ref, *, heads, chunks):
    qi = lax.broadcasted_iota(jnp.int32, (CHUNK, CHUNK), 0)
    si = lax.broadcasted_iota(jnp.int32, (CHUNK, CHUNK), 1)
    lower, upper = (si <= qi).astype(F32), (si >= qi).astype(F32)
    fwd_col = lax.broadcasted_iota(jnp.int32, (CHUNK, LANES), 1) < heads
    for c in range(chunks):
        rows = slice(c * CHUNK, (c + 1) * CHUNK)
        v = raw_ref[rows, :] + bias_ref[...]
        dt = jnp.maximum(v, 0.0) + jnp.log1p(jnp.exp(-jnp.abs(v)))
        a = dt * a_ref[...]
        pre, suf = _dot(lower, a, HI), _dot(upper, a, HI)
        cs = jnp.where(fwd_col, pre, suf)
        tot = jnp.where(fwd_col[0:1, :], pre[CHUNK - 1:CHUNK, :], suf[0:1, :])
        cs_ref[rows, :] = cs * LOG2E
        src_ref[rows, :] = ((cs - jnp.log(dt)) * LOG2E).T
        wst_ref[rows, :] = (jnp.exp(tot - cs) * dt).T
        etot_ref[c * 8:(c + 1) * 8, :] = jnp.broadcast_to(jnp.exp(tot), (8, LANES))


def ssd_decays(dt_raw, dt_bias, a_neg):
    t = dt_raw.shape[0]
    nc = t // CHUNK
    heads = dt_bias.shape[0] // 2
    chunks = _pick(nc, (10, 6, 5, 4, 3, 2, 1))
    pad = lambda vec: jnp.pad(vec.reshape(1, -1), ((0, 0), (0, LANES - vec.shape[0])))
    big = pl.BlockSpec((chunks * CHUNK, LANES), lambda i: (i, 0))
    vec = pl.BlockSpec((1, LANES), lambda i: (0, 0))
    full = jax.ShapeDtypeStruct((t, LANES), F32)
    return pl.pallas_call(
        functools.partial(_decay_kernel, heads=heads, chunks=chunks),
        grid=(nc // chunks,),
        in_specs=[big, vec, vec],
        out_specs=[big, big, big, pl.BlockSpec((chunks * 8, LANES), lambda i: (i, 0))],
        out_shape=[full, full, full, jax.ShapeDtypeStruct((nc * 8, LANES), F32)],
        compiler_params=_params("parallel"),
        name="ssd_decays",
    )(dt_raw, pad(dt_bias), pad(a_neg))


def _ssd_kernel(*refs, direction, heads, d_ssd, d_state, final):
    if final:
        x_ref, cs_ref, src_ref, wst_ref, etot_ref, yf_ref, z_ref, dsk_ref, g_ref, o_ref, state_ref = refs
    else:
        x_ref, cs_ref, src_ref, wst_ref, etot_ref, o_ref, state_ref = refs
    gp = d_ssd // SSD_GROUPS
    r_heads = gp // SSD_HEAD_DIM
    gn = SSD_GROUPS * d_state

    @pl.when(pl.program_id(0) == 0)
    def _():
        state_ref[...] = jnp.zeros_like(state_ref)

    qi = lax.broadcasted_iota(jnp.int32, (CHUNK, CHUNK), 0)
    si = lax.broadcasted_iota(jnp.int32, (CHUNK, CHUNK), 1)
    causal = (si <= qi) if direction == 0 else (si >= qi)
    cs = cs_ref[...]
    src_t = src_ref[...]
    wst_t = wst_ref[...]
    etot = etot_ref[0:1, :]
    lo = lax.broadcasted_iota(jnp.int32, (CHUNK, LANES), 1) < SSD_HEAD_DIM
    lo_row = lo[0:1, :]

    for g in range(SSD_GROUPS):
        h0 = direction * heads + g * r_heads
        bm = x_ref[:, d_ssd + g * d_state:d_ssd + (g + 1) * d_state]
        cm = x_ref[:, d_ssd + gn + g * d_state:d_ssd + gn + (g + 1) * d_state]
        cb = lax.dot_general(cm, bm, (((1,), (1,)), ((), ())), preferred_element_type=F32)
        bt = bm.astype(F32).T
        y_state = _dot(cm, state_ref[:, g * gp:(g + 1) * gp].astype(BF16))
        ys = []
        for k in range(r_heads // 2):
            ps = slice(g * gp + k * LANES, g * gp + (k + 1) * LANES)
            xd = x_ref[:, ps].astype(F32)
            y_acc = st_acc = ecs = None
            for half in range(2):
                h = h0 + 2 * k + half
                csq = jnp.broadcast_to(cs[:, h:h + 1], (CHUNK, CHUNK))
                m = (cb * jnp.exp2(jnp.where(causal, csq - src_t[h:h + 1, :], -1e30))).astype(BF16)
                xh = (jnp.where(lo, xd, 0.0) if half == 0 else jnp.where(lo, 0.0, xd)).astype(BF16)
                btw = (bt * wst_t[h:h + 1, :]).astype(BF16)
                py, pst = _dot(m, xh), _dot(btw, xh)
                e = jnp.exp2(csq)
                if half == 0:
                    y_acc, st_acc, ecs = py, pst, e
                else:
                    y_acc, st_acc, ecs = y_acc + py, st_acc + pst, jnp.where(lo, ecs, e)
            ys.append(y_acc + ecs * y_state[:, k * LANES:(k + 1) * LANES])
            h = h0 + 2 * k
            decay = jnp.where(lo_row, jnp.broadcast_to(etot[:, h:h + 1], (1, LANES)),
                              jnp.broadcast_to(etot[:, h + 1:h + 2], (1, LANES)))
            state_ref[:, ps] = state_ref[:, ps] * decay + st_acc

        gs = slice(g * gp, (g + 1) * gp)
        y = ys[0] if len(ys) == 1 else jnp.concatenate(ys, axis=1)
        if final:
            y = (yf_ref[:, gs] + y + dsk_ref[:, gs] * x_ref[:, gs].astype(F32)) * _silu(z_ref[:, gs].astype(F32))
            y = y * lax.rsqrt(jnp.mean(y * y, axis=-1, keepdims=True) + EPS) * g_ref[:, gs]
        o_ref[:, gs] = y.astype(o_ref.dtype)


def ssd_mixer(xbc, zx, dt_raw, dt_bias, a_neg, d_skip_x, norm_g, seq, d_ssd, d_state):
    t, c_xbc = xbc.shape
    nc = t // CHUNK
    nxc = seq // CHUNK
    ncc = nc - nxc
    heads = dt_bias.shape[0] // 2
    gp = d_ssd // SSD_GROUPS
    assert d_state == LANES and (gp // SSD_HEAD_DIM) % 2 == 0 and gp % LANES == 0 and 2 * heads <= LANES
    fwd = lambda j: jnp.where(j < ncc, nxc + j, j - ncc)
    bwd = lambda j: nc - 1 - j
    vec = lambda n: pl.BlockSpec((1, n), lambda j: (0, 0))
    decays = ssd_decays(dt_raw, dt_bias, a_neg)

    def call(direction, order, final, extra_in, extra_specs, out_dtype):
        per_chunk = pl.BlockSpec((CHUNK, LANES), lambda j: (order(j), 0))
        return pl.pallas_call(
            functools.partial(_ssd_kernel, direction=direction, heads=heads, d_ssd=d_ssd, d_state=d_state,
                              final=final),
            grid=(nc,),
            in_specs=[pl.BlockSpec((CHUNK, c_xbc), lambda j: (order(j), 0)),
                      per_chunk, per_chunk, per_chunk,
                      pl.BlockSpec((8, LANES), lambda j: (order(j), 0))] + extra_specs,
            out_specs=pl.BlockSpec((CHUNK, d_ssd), lambda j: (order(j), 0)),
            out_shape=jax.ShapeDtypeStruct((t, d_ssd), out_dtype),
            scratch_shapes=[pltpu.VMEM((d_state, d_ssd), F32)],
            compiler_params=_params("arbitrary"),
            name="ssd_bwd_out" if final else "ssd_fwd",
        )(xbc, *decays, *extra_in)

    y_f = call(0, fwd, False, [], [], F32)
    blk = pl.BlockSpec((CHUNK, d_ssd), lambda j: (bwd(j), 0))
    return call(1, bwd, True,
                [y_f, zx, d_skip_x.reshape(1, d_ssd), norm_g.reshape(1, d_ssd)],
                [blk, blk, vec(d_ssd), vec(d_ssd)], BF16)


def _sgu_kernel(u_ref, v_ref, g_ref, w_ref, bt_ref, o_ref, *, groups):
    for c in range(u_ref.shape[0] // CHUNK):
        rows = slice(c * CHUNK, (c + 1) * CHUNK)
        u = _gelu_tanh(u_ref[rows, :].astype(F32))
        v = _gelu_tanh(v_ref[rows, :].astype(F32))
        v = (v * lax.rsqrt(jnp.mean(v * v, axis=-1, keepdims=True) + EPS) * g_ref[...]).astype(BF16)
        for g in range(groups):
            sl = slice(g * GROUP_DIM, (g + 1) * GROUP_DIM)
            vm = _dot(w_ref[g], v[:, sl]) + bt_ref[:, g:g + 1]
            o_ref[rows, sl] = (u[:, sl] * vm).astype(o_ref.dtype)


def sgu(uvf, norm_g, w_s, b_s):
    t = uvf.shape[0]
    groups = w_s.shape[0]
    dm = groups * GROUP_DIM
    rb = CHUNK * _pick(t // CHUNK, (10, 5, 2, 1))
    return pl.pallas_call(
        functools.partial(_sgu_kernel, groups=groups),
        grid=(t // rb,),
        in_specs=[pl.BlockSpec((rb, dm), lambda c: (c, 0)),
                  pl.BlockSpec((rb, dm), lambda c: (c, 1)),
                  pl.BlockSpec((1, dm), lambda c: (0, 0)),
                  pl.BlockSpec((groups, CHUNK, CHUNK), lambda c: (0, 0, 0)),
                  pl.BlockSpec((CHUNK, groups), lambda c: (0, 0))],
        out_specs=pl.BlockSpec((rb, dm), lambda c: (c, 0)),
        out_shape=jax.ShapeDtypeStruct((t, dm), BF16),
        compiler_params=_params("parallel"),
        name="sgu",
    )(uvf, uvf, norm_g.reshape(1, dm), w_s.astype(BF16), b_s.T)


def _channel_dft(f, cs_tab):
    gr, gs = [], []
    for g in range(f.shape[1] // GROUP_DIM):
        p = _dot(f[:, g * GROUP_DIM:(g + 1) * GROUP_DIM], cs_tab)
        gr.append(p[:, :GROUP_DIM])
        gs.append(p[:, GROUP_DIM:])
    cat = lambda v: (v[0] if len(v) == 1 else jnp.concatenate(v, axis=1)).astype(BF16)
    return cat(gr), cat(gs)


def _fft_a_kernel(f_ref, m_ref, cs_ref, z_ref, *, n1):
    for a in range(f_ref.shape[0]):
        gr, gs = _channel_dft(f_ref[a], cs_ref[...])
        m = m_ref[a]
        z_ref[a] = (_dot(m[:, :n1], gr) + _dot(m[:, n1:], gs)).astype(z_ref.dtype)


def _fft_b_kernel(zr_ref, zi_ref, c_ref, s_ref, o_ref, *, scale):
    acc = _dot(c_ref[...], zr_ref[...]) - _dot(s_ref[...], zi_ref[...])
    o_ref[...] = (acc * scale).astype(o_ref.dtype)


def _fft_ctx_kernel(f_ref, c_ref, s_ref, cs_ref, o_ref, *, scale):
    gr, gs = _channel_dft(f_ref[...], cs_ref[...])
    acc = _dot(c_ref[...], gr) - _dot(s_ref[...], gs)
    o_ref[...] = (acc * scale).astype(o_ref.dtype)


def _cos_sin(n):
    ang = 2.0 * np.pi * (np.outer(np.arange(n), np.arange(n)) % n) / n
    return np.cos(ang), np.sin(ang)


def fourier_mix(uvf, col0, d_fft, seq, ctx_len):
    n2 = CHUNK
    n1 = seq // n2
    assert n1 * n2 == seq and n1 % 16 == 0 and d_fft % GROUP_DIM == 0
    cc, sc = _cos_sin(GROUP_DIM)
    cs_tab = jnp.asarray(np.concatenate([cc, sc], axis=1), BF16)
    c1, s1 = _cos_sin(n1)
    ang = 2.0 * np.pi * np.outer(np.arange(n2), np.arange(n1)) / seq
    ct, st = (jnp.asarray(v[:, :, None], F32) for v in (np.cos(ang), np.sin(ang)))
    c1, s1 = jnp.asarray(c1[None], F32), jnp.asarray(s1[None], F32)
    mc = c1 * ct - s1 * st
    ms = s1 * ct + c1 * st
    m_tab = jnp.concatenate([jnp.concatenate([mc, -ms], axis=2),
                             jnp.concatenate([ms, mc], axis=2)], axis=1).astype(BF16)
    c2, s2 = (jnp.asarray(v, BF16) for v in _cos_sin(n2))

    f_t = uvf[:seq, col0:col0 + d_fft].reshape(n1, n2, d_fft).transpose(1, 0, 2)
    ta = _pick(n2, (8, 4, 2, 1))
    z = pl.pallas_call(
        functools.partial(_fft_a_kernel, n1=n1),
        grid=(n2 // ta,),
        in_specs=[pl.BlockSpec((ta, n1, d_fft), lambda a: (a, 0, 0)),
                  pl.BlockSpec((ta, 2 * n1, 2 * n1), lambda a: (a, 0, 0)),
                  pl.BlockSpec((GROUP_DIM, 2 * GROUP_DIM), lambda a: (0, 0))],
        out_specs=pl.BlockSpec((ta, 2 * n1, d_fft), lambda a: (a, 0, 0)),
        out_shape=jax.ShapeDtypeStruct((n2, 2 * n1, d_fft), BF16),
        compiler_params=_params("parallel"),
        name="fft_stage_a",
    )(f_t, m_tab, cs_tab)

    width = n1 * d_fft
    tw = _pick(width, (8192, 4096, 2048, 1024, 512, 256, 128))
    nb = width // tw
    z2 = z.reshape(n2, 2 * width)
    tab = lambda: pl.BlockSpec((n2, n2), lambda j: (0, 0))
    y_x = pl.pallas_call(
        functools.partial(_fft_b_kernel, scale=float(1.0 / np.sqrt(seq * GROUP_DIM))),
        grid=(nb,),
        in_specs=[pl.BlockSpec((n2, tw), lambda j: (0, j)),
                  pl.BlockSpec((n2, tw), lambda j: (0, nb + j)),
                  tab(), tab()],
        out_specs=pl.BlockSpec((n2, tw), lambda j: (0, j)),
        out_shape=jax.ShapeDtypeStruct((n2, width), BF16),
        compiler_params=_params("parallel"),
        name="fft_stage_b",
    )(z2, z2, c2, s2).reshape(seq, d_fft)

    assert ctx_len == ROW_BLOCK and col0 % d_fft == 0
    cl, sl = (jnp.asarray(v, BF16) for v in _cos_sin(ctx_len))
    ctab = lambda: pl.BlockSpec((ctx_len, ctx_len), lambda j: (0, 0))
    y_c = pl.pallas_call(
        functools.partial(_fft_ctx_kernel, scale=float(1.0 / np.sqrt(ctx_len * GROUP_DIM))),
        grid=(1,),
        in_specs=[pl.BlockSpec((ctx_len, d_fft), lambda j: (seq // ctx_len, col0 // d_fft)),
                  ctab(), ctab(),
                  pl.BlockSpec((GROUP_DIM, 2 * GROUP_DIM), lambda j: (0, 0))],
        out_specs=pl.BlockSpec((ctx_len, d_fft), lambda j: (0, 0)),
        out_shape=jax.ShapeDtypeStruct((ctx_len, d_fft), BF16),
        compiler_params=_params("arbitrary"),
        name="fft_ctx",
    )(uvf, cl, sl, cs_tab)
    return jnp.concatenate([y_x, y_c], axis=0)


def kernel(x, c, ctx, c_ctx, ada_w1, ada_w2, ada_b, norm_ffn1, f1_w1, f1_w3, f1_w2, norm_mix, w_in, b_in,
           conv_w, conv_b, a_log, dt_bias, d_skip, norm_ssd, sgu_norm, sgu_w, sgu_b,
           w_br_ssd, w_br_mlp, w_br_fft, w_out, norm_ffn2, f2_w1, f2_w3, f2_w2, norm_final):
    batch, seq, d = x.shape
    ctx_len = ctx.shape[1]
    depth = ada_w1.shape[0]
    assert batch == 1 and seq % ROW_BLOCK == 0
    heads = a_log.shape[-1]
    d_ssd = norm_ssd.shape[-1]
    c_xbc = conv_w.shape[-1]
    d_state = (c_xbc - d_ssd) // (2 * SSD_GROUPS)
    d_mlp = sgu_norm.shape[-1]
    d_fft = w_br_fft.shape[1]
    assert heads * SSD_HEAD_DIM == d_ssd and d_mlp == d_fft
    off_xbc = d_ssd
    off_dt = off_xbc + c_xbc
    off_u = off_dt + 2 * heads
    off_gate = off_u + 2 * d_mlp + d_fft

    cv = jnp.zeros((8, d), F32).at[0].set(c[0]).at[1].set(c_ctx)
    mods = ada_mods(cv, ada_w1, ada_w2, ada_b)[:, 0:2, :].reshape(depth, 2, N_MOD, d)
    bf = lambda w: w.astype(BF16)
    f1_w1, f1_w3, f1_w2, f2_w1, f2_w3, f2_w2 = map(bf, (f1_w1, f1_w3, f1_w2, f2_w1, f2_w3, f2_w2))
    w_br_ssd, w_br_mlp, w_br_fft, w_out = map(bf, (w_br_ssd, w_br_mlp, w_br_fft, w_out))
    dt_pad = LANES - 2 * heads
    w_zx, b_zx = bf(w_in[:, :, :off_dt]), b_in[:, :off_dt]
    w_dt = bf(jnp.pad(w_in[:, :, off_dt:off_u], ((0, 0), (0, 0), (0, dt_pad))))
    b_dt = jnp.pad(b_in[:, off_dt:off_u], ((0, 0), (0, dt_pad)))
    w_rest, b_rest = bf(w_in[:, :, off_u:]), b_in[:, off_u:]
    n_uvf = off_gate - off_u

    xc = (x[0], ctx[0])
    for i in range(depth):
        sh1, sc1, g1, shm, scm, gm, sh2, sc2, g2 = (mods[i, :, k, :] for k in range(N_MOD))

        h = norm_mod(xc, norm_ffn1[i], sh1, sc1, seq)
        xc = down_residual(swiglu_up(h, f1_w1, f1_w3, i), f1_w2, i, xc, g1, 0.5, seq)

        h = norm_mod(xc, norm_mix[i], shm, scm, seq)
        zx = proj(h, w_zx, b_zx, i, 0, off_dt, BF16, name="proj_zx")
        dt_raw = proj(h, w_dt, b_dt, i, 0, LANES, F32, name="proj_dt")
        uvf = proj(h, w_rest, b_rest, i, 0, n_uvf, BF16, name="proj_uvf")
        gates = proj(h, w_rest, b_rest, i, n_uvf, 3 * d, BF16, sigmoid=True, name="proj_gates")

        xbc = conv_silu(zx, off_xbc, conv_w[i], conv_b[i], seq, ctx_len)
        a_neg = -jnp.exp(a_log[i].astype(F32)).reshape(-1)
        y_ssd = ssd_mixer(xbc, zx, dt_raw, dt_bias[i].reshape(-1), a_neg,
                          jnp.repeat(d_skip[i], SSD_HEAD_DIM), norm_ssd[i], seq, d_ssd, d_state)
        y_mlp = sgu(uvf, sgu_norm[i], sgu_w[i], sgu_b[i])
        y_fft = fourier_mix(uvf, 2 * d_mlp, d_fft, seq, ctx_len)

        merged = merge(y_ssd, y_mlp, y_fft, w_br_ssd, w_br_mlp, w_br_fft, i, gates)
        xc = down_residual(merged, w_out, i, xc, gm, 1.0, seq)

        h = norm_mod(xc, norm_ffn2[i], sh2, sc2, seq)
        xc = down_residual(swiglu_up(h, f2_w1, f2_w3, i), f2_w2, i, xc, g2, 0.5, seq)

    return final_norm(xc, norm_final, seq).reshape(1, seq, d)
```

```python
import functools

import numpy as np
import jax
import jax.numpy as jnp
from jax import lax
from jax.experimental import pallas as pl
from jax.experimental.pallas import tpu as pltpu

F32 = jnp.float32
BF16 = jnp.bfloat16
HI = lax.Precision.HIGHEST

EPS = 1e-6
GRID_W = 64
SSD_GROUPS = 4
SSD_HEAD_DIM = 64
CHUNK = 128
D_CONV = 5
GROUP_DIM = 128
N_MOD = 9
LANES = 128
ROW_BLOCK = 256
VMEM_LIMIT = 56 * 1024 * 1024


def _pick(n, cands):
    for c in cands:
        if n % c == 0:
            return c
    raise ValueError(f"no tile for {n} in {cands}")


def _params(*sem):
    return pltpu.CompilerParams(dimension_semantics=sem, vmem_limit_bytes=VMEM_LIMIT)


def _silu(v):
    return v * jax.nn.sigmoid(v)


def _gelu_tanh(v):
    return v * (0.5 * (1.0 + jnp.tanh(np.sqrt(2.0 / np.pi).astype(np.float32) * (v + 0.044715 * (v * v * v)))))


def _dot(a, b, precision=None):
    return jnp.dot(a, b, preferred_element_type=F32, precision=precision)


def _ada_kernel(cv_ref, w1_ref, w2_ref, b_ref, o_ref):
    t = _dot(_silu(cv_ref[...]), w1_ref[0], HI)
    o_ref[0] = _dot(t, w2_ref[0], HI) + b_ref[0]


def ada_mods(cv, w1, w2, b):
    nl, d, r = w1.shape
    n = w2.shape[-1]
    tn = _pick(n, (4608, 4096, 2304, 2048, 1024, 512, 256, 128))
    return pl.pallas_call(
        _ada_kernel,
        grid=(nl, n // tn),
        in_specs=[pl.BlockSpec((8, d), lambda l, j: (0, 0)),
                  pl.BlockSpec((1, d, r), lambda l, j: (l, 0, 0)),
                  pl.BlockSpec((1, r, tn), lambda l, j: (l, 0, j)),
                  pl.BlockSpec((1, 1, tn), lambda l, j: (l, 0, j))],
        out_specs=pl.BlockSpec((1, 8, tn), lambda l, j: (l, 0, j)),
        out_shape=jax.ShapeDtypeStruct((nl, 8, n), F32),
        compiler_params=_params("arbitrary", "arbitrary"),
        name="ada_mods",
    )(cv, w1, w2, b.reshape(nl, 1, n))


def _norm_mod_kernel(*refs, nx, split):
    if split:
        x_ref, c_ref, g_ref, sh_ref, sc_ref, o_ref = refs
    else:
        x_ref, g_ref, sh_ref, sc_ref, o_ref = refs

    def emit(src_ref):
        x = src_ref[...]
        y = x * lax.rsqrt(jnp.mean(x * x, axis=-1, keepdims=True) + EPS) * g_ref[...]
        o_ref[...] = (y * (1.0 + sc_ref[0]) + sh_ref[0]).astype(o_ref.dtype)

    if split:
        pl.when(pl.program_id(0) < nx)(lambda: emit(x_ref))
        pl.when(pl.program_id(0) >= nx)(lambda: emit(c_ref))
    else:
        emit(x_ref)


def norm_mod(xs, g, shift, scale, seq):
    split = isinstance(xs, tuple)
    d = g.shape[0]
    nx = seq // ROW_BLOCK
    sel = lambda i: (jnp.where(i >= nx, 1, 0), 0, 0)
    if split:
        t = seq + xs[1].shape[0]
        src_specs = [pl.BlockSpec((ROW_BLOCK, d), lambda i: (jnp.minimum(i, nx - 1), 0)),
                     pl.BlockSpec((ROW_BLOCK, d), lambda i: (jnp.maximum(i - nx, 0), 0))]
    else:
        xs = (xs,)
        t = xs[0].shape[0]
        src_specs = [pl.BlockSpec((ROW_BLOCK, d), lambda i: (i, 0))]
    return pl.pallas_call(
        functools.partial(_norm_mod_kernel, nx=nx, split=split),
        grid=(t // ROW_BLOCK,),
        in_specs=src_specs + [pl.BlockSpec((1, d), lambda i: (0, 0)),
                              pl.BlockSpec((1, 1, d), sel),
                              pl.BlockSpec((1, 1, d), sel)],
        out_specs=pl.BlockSpec((ROW_BLOCK, d), lambda i: (i, 0)),
        out_shape=jax.ShapeDtypeStruct((t, d), BF16),
        compiler_params=_params("parallel"),
        name="norm_mod",
    )(*xs, g.reshape(1, d), shift.reshape(2, 1, d), scale.reshape(2, 1, d))


def _final_norm_kernel(x_ref, g_ref, o_ref):
    x = x_ref[...]
    o_ref[...] = x * lax.rsqrt(jnp.mean(x * x, axis=-1, keepdims=True) + EPS) * g_ref[...]


def final_norm(xc, g, seq):
    d = xc.shape[1]
    return pl.pallas_call(
        _final_norm_kernel,
        grid=(seq // ROW_BLOCK,),
        in_specs=[pl.BlockSpec((ROW_BLOCK, d), lambda i: (i, 0)),
                  pl.BlockSpec((1, d), lambda i: (0, 0))],
        out_specs=pl.BlockSpec((ROW_BLOCK, d), lambda i: (i, 0)),
        out_shape=jax.ShapeDtypeStruct((seq, d), F32),
        compiler_params=_params("parallel"),
        name="final_norm",
    )(xc, g.reshape(1, d))


def _mm_tiles(t, n, wide=False):
    tm = _pick(t, (1280, 1024, 768, 640, 512, 384, 256))
    tn = _pick(n, ((1024,) if wide else ()) + (512, 384, 256, 128))
    return tm, tn


def _w_spec(k, tn, layer, col0=0):
    return pl.BlockSpec((None, k, tn), lambda i, j: (layer, 0, col0 + j))


def _up_kernel(h_ref, w1_ref, w3_ref, o_ref):
    h = h_ref[...]
    a = _dot(h, w1_ref[...])
    o_ref[...] = (_silu(a) * _dot(h, w3_ref[...])).astype(o_ref.dtype)


def swiglu_up(h, w1, w3, layer):
    t, d = h.shape
    n = w1.shape[2]
    tm, tn = _mm_tiles(t, n)
    return pl.pallas_call(
        _up_kernel,
        grid=(t // tm, n // tn),
        in_specs=[pl.BlockSpec((tm, d), lambda i, j: (i, 0)),
                  _w_spec(d, tn, layer), _w_spec(d, tn, layer)],
        out_specs=pl.BlockSpec((tm, tn), lambda i, j: (i, j)),
        out_shape=jax.ShapeDtypeStruct((t, n), BF16),
        compiler_params=_params("parallel", "arbitrary"),
        name="swiglu_up",
    )(h, w1, w3)


def _down_kernel(*refs, coef, tm, seq, n_latent_last):
    split = n_latent_last is not None
    if split:
        a_ref, w_ref, x_ref, c_ref, gate_ref, o_ref = refs
    else:
        a_ref, w_ref, x_ref, gate_ref, o_ref = refs
    i = pl.program_id(0)
    rows = i * tm + lax.broadcasted_iota(jnp.int32, (tm, 1), 0)
    gate = jnp.where(rows >= seq, gate_ref[1:2, :], gate_ref[0:1, :])
    upd = (coef * gate) * _dot(a_ref[...], w_ref[...])
    if not split:
        o_ref[...] = x_ref[...] + upd
        return
    last = pl.num_programs(0) - 1

    @pl.when(i < last)
    def _():
        o_ref[...] = x_ref[...] + upd

    @pl.when(i == last)
    def _():
        o_ref[:n_latent_last, :] = x_ref[:n_latent_last, :] + upd[:n_latent_last, :]
        o_ref[n_latent_last:, :] = c_ref[...] + upd[n_latent_last:, :]


def down_residual(a, w, layer, res, gate, coef, seq):
    t, k = a.shape
    n = w.shape[2]
    tm, tn = _mm_tiles(t, n)
    split = isinstance(res, tuple)
    res_spec = pl.BlockSpec((tm, tn), lambda i, j: (i, j))
    if split:
        ctx_len = res[1].shape[0]
        n_latent_last = tm - ctx_len
        assert res[0].shape[0] == seq and seq + ctx_len == t and 0 < n_latent_last and n_latent_last % 8 == 0
        res_specs = [res_spec, pl.BlockSpec((ctx_len, tn), lambda i, j: (0, j))]
    else:
        res, n_latent_last = (res,), None
        res_specs = [res_spec]
    return pl.pallas_call(
        functools.partial(_down_kernel, coef=coef, tm=tm, seq=seq, n_latent_last=n_latent_last),
        grid=(t // tm, n // tn),
        in_specs=[pl.BlockSpec((tm, k), lambda i, j: (i, 0)), _w_spec(k, tn, layer)] + res_specs
                 + [pl.BlockSpec((2, tn), lambda i, j: (0, j))],
        out_specs=pl.BlockSpec((tm, tn), lambda i, j: (i, j)),
        out_shape=jax.ShapeDtypeStruct((t, n), F32),
        input_output_aliases={} if split else {2: 0},
        compiler_params=_params("parallel", "arbitrary"),
        name="down_residual",
    )(a, w, *res, gate)


def _proj_kernel(h_ref, wt_ref, b_ref, o_ref, *, sigmoid):
    acc = lax.dot_general(h_ref[...], wt_ref[...], (((1,), (1,)), ((), ())), preferred_element_type=F32) + b_ref[...]
    if sigmoid:
        acc = jax.nn.sigmoid(acc)
    o_ref[...] = acc.astype(o_ref.dtype)


def proj(h, wt, b, layer, col0, n, out_dtype, sigmoid=False, name="proj"):
    t, d = h.shape
    tm, tn = _mm_tiles(t, n, wide=True)
    nl, nw, _ = wt.shape
    row0 = layer * nw + col0
    assert tn % LANES == 0 and n % tn == 0 and row0 % 16 == 0
    return pl.pallas_call(
        functools.partial(_proj_kernel, sigmoid=sigmoid),
        grid=(t // tm, n // tn),
        in_specs=[pl.BlockSpec((tm, d), lambda i, j: (i, 0)),
                  pl.BlockSpec((pl.Element(tn), pl.Element(d)), lambda i, j: (pl.multiple_of(row0 + j * tn, 16), 0)),
                  pl.BlockSpec((1, tn), lambda i, j: (0, j))],
        out_specs=pl.BlockSpec((tm, tn), lambda i, j: (i, j)),
        out_shape=jax.ShapeDtypeStruct((t, n), out_dtype),
        compiler_params=_params("parallel", "arbitrary"),
        name=name,
    )(h, wt.reshape(nl * nw, d), b[:n].reshape(1, n))


def _merge_kernel(ys_ref, ym_ref, yf_ref, ws_ref, wm_ref, wf_ref, gs_ref, gm_ref, gf_ref, o_ref):
    acc = gs_ref[...].astype(F32) * _dot(ys_ref[...], ws_ref[...])
    acc = acc + gm_ref[...].astype(F32) * _dot(ym_ref[...], wm_ref[...])
    acc = acc + gf_ref[...].astype(F32) * _dot(yf_ref[...], wf_ref[...])
    o_ref[...] = acc.astype(o_ref.dtype)


def merge(y_ssd, y_mlp, y_fft, w_ssd, w_mlp, w_fft, layer, gates):
    t = y_ssd.shape[0]
    n = w_ssd.shape[2]
    tm, tn = _mm_tiles(t, n)
    nb = n // tn
    row = lambda k: pl.BlockSpec((tm, k), lambda i, j: (i, 0))
    col = lambda k: _w_spec(k, tn, layer)
    gate = lambda b: pl.BlockSpec((tm, tn), lambda i, j: (i, b * nb + j))
    return pl.pallas_call(
        _merge_kernel,
        grid=(t // tm, nb),
        in_specs=[row(y_ssd.shape[1]), row(y_mlp.shape[1]), row(y_fft.shape[1]),
                  col(w_ssd.shape[1]), col(w_mlp.shape[1]), col(w_fft.shape[1]),
                  gate(0), gate(1), gate(2)],
        out_specs=pl.BlockSpec((tm, tn), lambda i, j: (i, j)),
        out_shape=jax.ShapeDtypeStruct((t, n), BF16),
        compiler_params=_params("parallel", "arbitrary"),
        name="merge",
    )(y_ssd, y_mlp, y_fft, w_ssd, w_mlp, w_fft, gates, gates, gates)


def _conv_kernel(u_ref, w_ref, b_ref, o_ref, *, nx, ctx_len):
    u = u_ref[...].astype(F32)
    rb = u.shape[0]
    width = jnp.where(pl.program_id(0) >= nx, ctx_len, GRID_W)
    pos = lax.broadcasted_iota(jnp.int32, (rb, 1), 0) & (width - 1)
    y = jnp.broadcast_to(b_ref[...], u.shape)
    for k in range(D_CONV):
        off = k - D_CONV // 2
        shifted = u if off == 0 else pltpu.roll(u, (-off) % rb, 0)
        valid = (pos + off >= 0) & (pos + off < width)
        y = y + jnp.where(valid, shifted, 0.0) * w_ref[k:k + 1, :]
    o_ref[...] = _silu(y).astype(o_ref.dtype)


def conv_silu(zx, col0, conv_w, conv_b, seq, ctx_len):
    t = zx.shape[0]
    c = conv_w.shape[1]
    tc = next(v for v in (1024, 512, 256, 128) if c % v == 0 and col0 % v == 0)
    assert ctx_len == ROW_BLOCK and ROW_BLOCK % GRID_W == 0
    assert GRID_W & (GRID_W - 1) == 0 and ctx_len & (ctx_len - 1) == 0
    nx = seq // ROW_BLOCK
    return pl.pallas_call(
        functools.partial(_conv_kernel, nx=nx, ctx_len=ctx_len),
        grid=(t // ROW_BLOCK, c // tc),
        in_specs=[pl.BlockSpec((ROW_BLOCK, tc), lambda i, j: (i, col0 // tc + j)),
                  pl.BlockSpec((D_CONV, tc), lambda i, j: (0, j)),
                  pl.BlockSpec((1, tc), lambda i, j: (0, j))],
        out_specs=pl.BlockSpec((ROW_BLOCK, tc), lambda i, j: (i, j)),
        out_shape=jax.ShapeDtypeStruct((t, c), BF16),
        compiler_params=_params("parallel", "arbitrary"),
        name="conv_silu",
    )(zx, conv_w, conv_b.reshape(1, c))


LOG2E = float(np.log2(np.e))


def _decay_kernel(raw_ref, bias_ref, a_ref, cs_ref, src_ref, wst_ref, etot_ref, *, heads, chunks):
    qi = lax.broadcasted_iota(jnp.int32, (CHUNK, CHUNK), 0)
    si = lax.broadcasted_iota(jnp.int32, (CHUNK, CHUNK), 1)
    lower, upper = (si <= qi).astype(F32), (si >= qi).astype(F32)
    fwd_col = lax.broadcasted_iota(jnp.int32, (CHUNK, LANES), 1) < heads
    for c in range(chunks):
        rows = slice(c * CHUNK, (c + 1) * CHUNK)
        v = raw_ref[rows, :] + bias_ref[...]
        dt = jnp.maximum(v, 0.0) + jnp.log1p(jnp.exp(-jnp.abs(v)))
        a = dt * a_ref[...]
        pre, suf = _dot(lower, a, HI), _dot(upper, a, HI)
        cs = jnp.where(fwd_col, pre, suf)
        tot = jnp.where(fwd_col[0:1, :], pre[CHUNK - 1:CHUNK, :], suf[0:1, :])
        cs_ref[rows, :] = cs * LOG2E
        src_ref[rows, :] = ((cs - jnp.log(dt)) * LOG2E).T
        wst_ref[rows, :] = (jnp.exp(tot - cs) * dt).T
        etot_ref[c * 8:(c + 1) * 8, :] = jnp.broadcast_to(jnp.exp(tot), (8, LANES))


def ssd_decays(dt_raw, dt_bias, a_neg):
    t = dt_raw.shape[0]
    nc = t // CHUNK
    heads = dt_bias.shape[0] // 2
    chunks = _pick(nc, (10, 6, 5, 4, 3, 2, 1))
    pad = lambda vec: jnp.pad(vec.reshape(1, -1), ((0, 0), (0, LANES - vec.shape[0])))
    big = pl.BlockSpec((chunks * CHUNK, LANES), lambda i: (i, 0))
    vec = pl.BlockSpec((1, LANES), lambda i: (0, 0))
    full = jax.ShapeDtypeStruct((t, LANES), F32)
    return pl.pallas_call(
        functools.partial(_decay_kernel, heads=heads, chunks=chunks),
        grid=(nc // chunks,),
        in_specs=[big, vec, vec],
        out_specs=[big, big, big, pl.BlockSpec((chunks * 8, LANES), lambda i: (i, 0))],
        out_shape=[full, full, full, jax.ShapeDtypeStruct((nc * 8, LANES), F32)],
        compiler_params=_params("parallel"),
        name="ssd_decays",
    )(dt_raw, pad(dt_bias), pad(a_neg))


def _ssd_kernel(*refs, direction, heads, d_ssd, d_state, final):
    if final:
        x_ref, cs_ref, src_ref, wst_ref, etot_ref, yf_ref, z_ref, dsk_ref, g_ref, o_ref, state_ref = refs
    else:
        x_ref, cs_ref, src_ref, wst_ref, etot_ref, o_ref, state_ref = refs
    gp = d_ssd // SSD_GROUPS
    r_heads = gp // SSD_HEAD_DIM
    gn = SSD_GROUPS * d_state

    @pl.when(pl.program_id(0) == 0)
    def _():
        state_ref[...] = jnp.zeros_like(state_ref)

    qi = lax.broadcasted_iota(jnp.int32, (CHUNK, CHUNK), 0)
    si = lax.broadcasted_iota(jnp.int32, (CHUNK, CHUNK), 1)
    causal = (si <= qi) if direction == 0 else (si >= qi)
    cs = cs_ref[...]
    src_t = src_ref[...]
    wst_t = wst_ref[...]
    etot = etot_ref[0:1, :]
    lo = lax.broadcasted_iota(jnp.int32, (CHUNK, LANES), 1) < SSD_HEAD_DIM
    lo_row = lo[0:1, :]

    for g in range(SSD_GROUPS):
        h0 = direction * heads + g * r_heads
        bm = x_ref[:, d_ssd + g * d_state:d_ssd + (g + 1) * d_state]
        cm = x_ref[:, d_ssd + gn + g * d_state:d_ssd + gn + (g + 1) * d_state]
        cb = lax.dot_general(cm, bm, (((1,), (1,)), ((), ())), preferred_element_type=F32)
        bt = bm.astype(F32).T
        y_state = _dot(cm, state_ref[:, g * gp:(g + 1) * gp].astype(BF16))
        ys = []
        for k in range(r_heads // 2):
            ps = slice(g * gp + k * LANES, g * gp + (k + 1) * LANES)
            xd = x_ref[:, ps].astype(F32)
            y_acc = st_acc = ecs = None
            for half in range(2):
                h = h0 + 2 * k + half
                csq = jnp.broadcast_to(cs[:, h:h + 1], (CHUNK, CHUNK))
                m = (cb * jnp.exp2(jnp.where(causal, csq - src_t[h:h + 1, :], -1e30))).astype(BF16)
                xh = (jnp.where(lo, xd, 0.0) if half == 0 else jnp.where(lo, 0.0, xd)).astype(BF16)
                btw = (bt * wst_t[h:h + 1, :]).astype(BF16)
                py, pst = _dot(m, xh), _dot(btw, xh)
                e = jnp.exp2(csq)
                if half == 0:
                    y_acc, st_acc, ecs = py, pst, e
                else:
                    y_acc, st_acc, ecs = y_acc + py, st_acc + pst, jnp.where(lo, ecs, e)
            ys.append(y_acc + ecs * y_state[:, k * LANES:(k + 1) * LANES])
            h = h0 + 2 * k
            decay = jnp.where(lo_row, jnp.broadcast_to(etot[:, h:h + 1], (1, LANES)),
                              jnp.broadcast_to(etot[:, h + 1:h + 2], (1, LANES)))
            state_ref[:, ps] = state_ref[:, ps] * decay + st_acc

        gs = slice(g * gp, (g + 1) * gp)
        y = ys[0] if len(ys) == 1 else jnp.concatenate(ys, axis=1)
        if final:
            y = (yf_ref[:, gs] + y + dsk_ref[:, gs] * x_ref[:, gs].astype(F32)) * _silu(z_ref[:, gs].astype(F32))
            y = y * lax.rsqrt(jnp.mean(y * y, axis=-1, keepdims=True) + EPS) * g_ref[:, gs]
        o_ref[:, gs] = y.astype(o_ref.dtype)


def ssd_mixer(xbc, zx, dt_raw, dt_bias, a_neg, d_skip_x, norm_g, seq, d_ssd, d_state):
    t, c_xbc = xbc.shape
    nc = t // CHUNK
    nxc = seq // CHUNK
    ncc = nc - nxc
    heads = dt_bias.shape[0] // 2
    gp = d_ssd // SSD_GROUPS
    assert d_state == LANES and (gp // SSD_HEAD_DIM) % 2 == 0 and gp % LANES == 0 and 2 * heads <= LANES
    fwd = lambda j: jnp.where(j < ncc, nxc + j, j - ncc)
    bwd = lambda j: nc - 1 - j
    vec = lambda n: pl.BlockSpec((1, n), lambda j: (0, 0))
    decays = ssd_decays(dt_raw, dt_bias, a_neg)

    def call(direction, order, final, extra_in, extra_specs, out_dtype):
        per_chunk = pl.BlockSpec((CHUNK, LANES), lambda j: (order(j), 0))
        return pl.pallas_call(
            functools.partial(_ssd_kernel, direction=direction, heads=heads, d_ssd=d_ssd, d_state=d_state,
                              final=final),
            grid=(nc,),
            in_specs=[pl.BlockSpec((CHUNK, c_xbc), lambda j: (order(j), 0)),
                      per_chunk, per_chunk, per_chunk,
                      pl.BlockSpec((8, LANES), lambda j: (order(j), 0))] + extra_specs,
            out_specs=pl.BlockSpec((CHUNK, d_ssd), lambda j: (order(j), 0)),
            out_shape=jax.ShapeDtypeStruct((t, d_ssd), out_dtype),
            scratch_shapes=[pltpu.VMEM((d_state, d_ssd), F32)],
            compiler_params=_params("arbitrary"),
            name="ssd_bwd_out" if final else "ssd_fwd",
        )(xbc, *decays, *extra_in)

    y_f = call(0, fwd, False, [], [], F32)
    blk = pl.BlockSpec((CHUNK, d_ssd), lambda j: (bwd(j), 0))
    return call(1, bwd, True,
                [y_f, zx, d_skip_x.reshape(1, d_ssd), norm_g.reshape(1, d_ssd)],
                [blk, blk, vec(d_ssd), vec(d_ssd)], BF16)


def _sgu_kernel(u_ref, v_ref, g_ref, w_ref, bt_ref, o_ref, *, groups):
    for c in range(u_ref.shape[0] // CHUNK):
        rows = slice(c * CHUNK, (c + 1) * CHUNK)
        u = _gelu_tanh(u_ref[rows, :].astype(F32))
        v = _gelu_tanh(v_ref[rows, :].astype(F32))
        v = (v * lax.rsqrt(jnp.mean(v * v, axis=-1, keepdims=True) + EPS) * g_ref[...]).astype(BF16)
        for g in range(groups):
            sl = slice(g * GROUP_DIM, (g + 1) * GROUP_DIM)
            vm = _dot(w_ref[g], v[:, sl]) + bt_ref[:, g:g + 1]
            o_ref[rows, sl] = (u[:, sl] * vm).astype(o_ref.dtype)


def sgu(uvf, norm_g, w_s, b_s):
    t = uvf.shape[0]
    groups = w_s.shape[0]
    dm = groups * GROUP_DIM
    rb = CHUNK * _pick(t // CHUNK, (10, 5, 2, 1))
    return pl.pallas_call(
        functools.partial(_sgu_kernel, groups=groups),
        grid=(t // rb,),
        in_specs=[pl.BlockSpec((rb, dm), lambda c: (c, 0)),
                  pl.BlockSpec((rb, dm), lambda c: (c, 1)),
                  pl.BlockSpec((1, dm), lambda c: (0, 0)),
                  pl.BlockSpec((groups, CHUNK, CHUNK), lambda c: (0, 0, 0)),
                  pl.BlockSpec((CHUNK, groups), lambda c: (0, 0))],
        out_specs=pl.BlockSpec((rb, dm), lambda c: (c, 0)),
        out_shape=jax.ShapeDtypeStruct((t, dm), BF16),
        compiler_params=_params("parallel"),
        name="sgu",
    )(uvf, uvf, norm_g.reshape(1, dm), w_s.astype(BF16), b_s.T)


def _channel_dft(f, cs_tab):
    gr, gs = [], []
    for g in range(f.shape[1] // GROUP_DIM):
        p = _dot(f[:, g * GROUP_DIM:(g + 1) * GROUP_DIM], cs_tab)
        gr.append(p[:, :GROUP_DIM])
        gs.append(p[:, GROUP_DIM:])
    cat = lambda v: (v[0] if len(v) == 1 else jnp.concatenate(v, axis=1)).astype(BF16)
    return cat(gr), cat(gs)


def _fft_a_kernel(f_ref, m_ref, cs_ref, z_ref, *, n1):
    for a in range(f_ref.shape[0]):
        gr, gs = _channel_dft(f_ref[a], cs_ref[...])
        m = m_ref[a]
        z_ref[a] = (_dot(m[:, :n1], gr) + _dot(m[:, n1:], gs)).astype(z_ref.dtype)


def _fft_b_kernel(zr_ref, zi_ref, c_ref, s_ref, o_ref, *, scale):
    acc = _dot(c_ref[...], zr_ref[...]) - _dot(s_ref[...], zi_ref[...])
    o_ref[...] = (acc * scale).astype(o_ref.dtype)


def _fft_ctx_kernel(f_ref, c_ref, s_ref, cs_ref, o_ref, *, scale):
    gr, gs = _channel_dft(f_ref[...], cs_ref[...])
    acc = _dot(c_ref[...], gr) - _dot(s_ref[...], gs)
    o_ref[...] = (acc * scale).astype(o_ref.dtype)


def _cos_sin(n):
    ang = 2.0 * np.pi * (np.outer(np.arange(n), np.arange(n)) % n) / n
    return np.cos(ang), np.sin(ang)


def fourier_mix(uvf, col0, d_fft, seq, ctx_len):
    n2 = CHUNK
    n1 = seq // n2
    assert n1 * n2 == seq and n1 % 16 == 0 and d_fft % GROUP_DIM == 0
    cc, sc = _cos_sin(GROUP_DIM)
    cs_tab = jnp.asarray(np.concatenate([cc, sc], axis=1), BF16)
    c1, s1 = _cos_sin(n1)
    ang = 2.0 * np.pi * np.outer(np.arange(n2), np.arange(n1)) / seq
    ct, st = (jnp.asarray(v[:, :, None], F32) for v in (np.cos(ang), np.sin(ang)))
    c1, s1 = jnp.asarray(c1[None], F32), jnp.asarray(s1[None], F32)
    mc = c1 * ct - s1 * st
    ms = s1 * ct + c1 * st
    m_tab = jnp.concatenate([jnp.concatenate([mc, -ms], axis=2),
                             jnp.concatenate([ms, mc], axis=2)], axis=1).astype(BF16)
    c2, s2 = (jnp.asarray(v, BF16) for v in _cos_sin(n2))

    f_t = uvf[:seq, col0:col0 + d_fft].reshape(n1, n2, d_fft).transpose(1, 0, 2)
    ta = _pick(n2, (8, 4, 2, 1))
    z = pl.pallas_call(
        functools.partial(_fft_a_kernel, n1=n1),
        grid=(n2 // ta,),
        in_specs=[pl.BlockSpec((ta, n1, d_fft), lambda a: (a, 0, 0)),
                  pl.BlockSpec((ta, 2 * n1, 2 * n1), lambda a: (a, 0, 0)),
                  pl.BlockSpec((GROUP_DIM, 2 * GROUP_DIM), lambda a: (0, 0))],
        out_specs=pl.BlockSpec((ta, 2 * n1, d_fft), lambda a: (a, 0, 0)),
        out_shape=jax.ShapeDtypeStruct((n2, 2 * n1, d_fft), BF16),
        compiler_params=_params("parallel"),
        name="fft_stage_a",
    )(f_t, m_tab, cs_tab)

    width = n1 * d_fft
    tw = _pick(width, (8192, 4096, 2048, 1024, 512, 256, 128))
    nb = width // tw
    z2 = z.reshape(n2, 2 * width)
    tab = lambda: pl.BlockSpec((n2, n2), lambda j: (0, 0))
    y_x = pl.pallas_call(
        functools.partial(_fft_b_kernel, scale=float(1.0 / np.sqrt(seq * GROUP_DIM))),
        grid=(nb,),
        in_specs=[pl.BlockSpec((n2, tw), lambda j: (0, j)),
                  pl.BlockSpec((n2, tw), lambda j: (0, nb + j)),
                  tab(), tab()],
        out_specs=pl.BlockSpec((n2, tw), lambda j: (0, j)),
        out_shape=jax.ShapeDtypeStruct((n2, width), BF16),
        compiler_params=_params("parallel"),
        name="fft_stage_b",
    )(z2, z2, c2, s2).reshape(seq, d_fft)

    assert ctx_len == ROW_BLOCK and col0 % d_fft == 0
    cl, sl = (jnp.asarray(v, BF16) for v in _cos_sin(ctx_len))
    ctab = lambda: pl.BlockSpec((ctx_len, ctx_len), lambda j: (0, 0))
    y_c = pl.pallas_call(
        functools.partial(_fft_ctx_kernel, scale=float(1.0 / np.sqrt(ctx_len * GROUP_DIM))),
        grid=(1,),
        in_specs=[pl.BlockSpec((ctx_len, d_fft), lambda j: (seq // ctx_len, col0 // d_fft)),
                  ctab(), ctab(),
                  pl.BlockSpec((GROUP_DIM, 2 * GROUP_DIM), lambda j: (0, 0))],
        out_specs=pl.BlockSpec((ctx_len, d_fft), lambda j: (0, 0)),
        out_shape=jax.ShapeDtypeStruct((ctx_len, d_fft), BF16),
        compiler_params=_params("arbitrary"),
        name="fft_ctx",
    )(uvf, cl, sl, cs_tab)
    return jnp.concatenate([y_x, y_c], axis=0)


def kernel(x, c, ctx, c_ctx, ada_w1, ada_w2, ada_b, norm_ffn1, f1_w1, f1_w3, f1_w2, norm_mix, w_in, b_in,
           conv_w, conv_b, a_log, dt_bias, d_skip, norm_ssd, sgu_norm, sgu_w, sgu_b,
           w_br_ssd, w_br_mlp, w_br_fft, w_out, norm_ffn2, f2_w1, f2_w3, f2_w2, norm_final):
    batch, seq, d = x.shape
    ctx_len = ctx.shape[1]
    depth = ada_w1.shape[0]
    assert batch == 1 and seq % ROW_BLOCK == 0
    heads = a_log.shape[-1]
    d_ssd = norm_ssd.shape[-1]
    c_xbc = conv_w.shape[-1]
    d_state = (c_xbc - d_ssd) // (2 * SSD_GROUPS)
    d_mlp = sgu_norm.shape[-1]
    d_fft = w_br_fft.shape[1]
    assert heads * SSD_HEAD_DIM == d_ssd and d_mlp == d_fft
    off_xbc = d_ssd
    off_dt = off_xbc + c_xbc
    off_u = off_dt + 2 * heads
    off_gate = off_u + 2 * d_mlp + d_fft

    cv = jnp.zeros((8, d), F32).at[0].set(c[0]).at[1].set(c_ctx)
    mods = ada_mods(cv, ada_w1, ada_w2, ada_b)[:, 0:2, :].reshape(depth, 2, N_MOD, d)
    bf = lambda w: w.astype(BF16)
    f1_w1, f1_w3, f1_w2, f2_w1, f2_w3, f2_w2 = map(bf, (f1_w1, f1_w3, f1_w2, f2_w1, f2_w3, f2_w2))
    w_br_ssd, w_br_mlp, w_br_fft, w_out = map(bf, (w_br_ssd, w_br_mlp, w_br_fft, w_out))
    w_in_t = bf(jnp.swapaxes(w_in, 1, 2))
    b_dt = jnp.pad(b_in[:, off_dt:off_u], ((0, 0), (0, LANES - 2 * heads)))
    n_uvf = off_gate - off_u

    xc = (x[0], ctx[0])
    for i in range(depth):
        sh1, sc1, g1, shm, scm, gm, sh2, sc2, g2 = (mods[i, :, k, :] for k in range(N_MOD))

        h = norm_mod(xc, norm_ffn1[i], sh1, sc1, seq)
        xc = down_residual(swiglu_up(h, f1_w1, f1_w3, i), f1_w2, i, xc, g1, 0.5, seq)

        h = norm_mod(xc, norm_mix[i], shm, scm, seq)
        zx = proj(h, w_in_t, b_in[i], i, 0, off_dt, BF16, name="proj_zx")
        dt_raw = proj(h, w_in_t, b_dt[i], i, off_dt, LANES, F32, name="proj_dt")
        uvf = proj(h, w_in_t, b_in[i, off_u:], i, off_u, n_uvf, BF16, name="proj_uvf")
        gates = proj(h, w_in_t, b_in[i, off_gate:], i, off_gate, 3 * d, BF16, sigmoid=True, name="proj_gates")

        xbc = conv_silu(zx, off_xbc, conv_w[i], conv_b[i], seq, ctx_len)
        a_neg = -jnp.exp(a_log[i].astype(F32)).reshape(-1)
        y_ssd = ssd_mixer(xbc, zx, dt_raw, dt_bias[i].reshape(-1), a_neg,
                          jnp.repeat(d_skip[i], SSD_HEAD_DIM), norm_ssd[i], seq, d_ssd, d_state)
        y_mlp = sgu(uvf, sgu_norm[i], sgu_w[i], sgu_b[i])
        y_fft = fourier_mix(uvf, 2 * d_mlp, d_fft, seq, ctx_len)

        merged = merge(y_ssd, y_mlp, y_fft, w_br_ssd, w_br_mlp, w_br_fft, i, gates)
        xc = down_residual(merged, w_out, i, xc, gm, 1.0, seq)

        h = norm_mod(xc, norm_ffn2[i], sh2, sc2, seq)
        xc = down_residual(swiglu_up(h, f2_w1, f2_w3, i), f2_w2, i, xc, g2, 0.5, seq)

    return final_norm(xc, norm_final, seq).reshape(1, seq, d)
```

```python
import functools

import numpy as np
import jax
import jax.numpy as jnp
from jax import lax
from jax.experimental import pallas as pl
from jax.experimental.pallas import tpu as pltpu

F32 = jnp.float32
BF16 = jnp.bfloat16
HI = lax.Precision.HIGHEST

EPS = 1e-6
GRID_W = 64
SSD_GROUPS = 4
SSD_HEAD_DIM = 64
CHUNK = 128
D_CONV = 5
GROUP_DIM = 128
N_MOD = 9
LANES = 128
ROW_BLOCK = 256
VMEM_LIMIT = 56 * 1024 * 1024


def _pick(n, cands):
    for c in cands:
        if n % c == 0:
            return c
    raise ValueError(f"no tile for {n} in {cands}")


def _params(*sem):
    return pltpu.CompilerParams(dimension_semantics=sem, vmem_limit_bytes=VMEM_LIMIT)


def _silu(v):
    return v * jax.nn.sigmoid(v)


def _gelu_tanh(v):
    return v * (0.5 * (1.0 + jnp.tanh(np.sqrt(2.0 / np.pi).astype(np.float32) * (v + 0.044715 * (v * v * v)))))


def _dot(a, b, precision=None):
    return jnp.dot(a, b, preferred_element_type=F32, precision=precision)


def _ada_kernel(cv_ref, w1_ref, w2_ref, b_ref, o_ref):
    t = _dot(_silu(cv_ref[...]), w1_ref[0], HI)
    o_ref[0] = _dot(t, w2_ref[0], HI) + b_ref[0]


def ada_mods(cv, w1, w2, b):
    nl, d, r = w1.shape
    n = w2.shape[-1]
    tn = _pick(n, (4608, 4096, 2304, 2048, 1024, 512, 256, 128))
    return pl.pallas_call(
        _ada_kernel,
        grid=(nl, n // tn),
        in_specs=[pl.BlockSpec((8, d), lambda l, j: (0, 0)),
                  pl.BlockSpec((1, d, r), lambda l, j: (l, 0, 0)),
                  pl.BlockSpec((1, r, tn), lambda l, j: (l, 0, j)),
                  pl.BlockSpec((1, 1, tn), lambda l, j: (l, 0, j))],
        out_specs=pl.BlockSpec((1, 8, tn), lambda l, j: (l, 0, j)),
        out_shape=jax.ShapeDtypeStruct((nl, 8, n), F32),
        compiler_params=_params("arbitrary", "arbitrary"),
        name="ada_mods",
    )(cv, w1, w2, b.reshape(nl, 1, n))


def _norm_mod_kernel(*refs, nx, split):
    if split:
        x_ref, c_ref, g_ref, sh_ref, sc_ref, o_ref = refs
    else:
        x_ref, g_ref, sh_ref, sc_ref, o_ref = refs

    def emit(src_ref):
        x = src_ref[...].astype(F32)
        y = x * lax.rsqrt(jnp.mean(x * x, axis=-1, keepdims=True) + EPS) * g_ref[...]
        o_ref[...] = (y * (1.0 + sc_ref[0]) + sh_ref[0]).astype(o_ref.dtype)

    if split:
        pl.when(pl.program_id(0) < nx)(lambda: emit(x_ref))
        pl.when(pl.program_id(0) >= nx)(lambda: emit(c_ref))
    else:
        emit(x_ref)


def norm_mod(xs, g, shift, scale, seq):
    split = isinstance(xs, tuple)
    d = g.shape[0]
    nx = seq // ROW_BLOCK
    sel = lambda i: (jnp.where(i >= nx, 1, 0), 0, 0)
    if split:
        t = seq + xs[1].shape[0]
        src_specs = [pl.BlockSpec((ROW_BLOCK, d), lambda i: (jnp.minimum(i, nx - 1), 0)),
                     pl.BlockSpec((ROW_BLOCK, d), lambda i: (jnp.maximum(i - nx, 0), 0))]
    else:
        xs = (xs,)
        t = xs[0].shape[0]
        src_specs = [pl.BlockSpec((ROW_BLOCK, d), lambda i: (i, 0))]
    return pl.pallas_call(
        functools.partial(_norm_mod_kernel, nx=nx, split=split),
        grid=(t // ROW_BLOCK,),
        in_specs=src_specs + [pl.BlockSpec((1, d), lambda i: (0, 0)),
                              pl.BlockSpec((1, 1, d), sel),
                              pl.BlockSpec((1, 1, d), sel)],
        out_specs=pl.BlockSpec((ROW_BLOCK, d), lambda i: (i, 0)),
        out_shape=jax.ShapeDtypeStruct((t, d), BF16),
        compiler_params=_params("parallel"),
        name="norm_mod",
    )(*xs, g.reshape(1, d), shift.reshape(2, 1, d), scale.reshape(2, 1, d))


def _final_norm_kernel(x_ref, g_ref, o_ref):
    x = x_ref[...]
    o_ref[...] = x * lax.rsqrt(jnp.mean(x * x, axis=-1, keepdims=True) + EPS) * g_ref[...]


def final_norm(xc, g, seq):
    d = xc.shape[1]
    return pl.pallas_call(
        _final_norm_kernel,
        grid=(seq // ROW_BLOCK,),
        in_specs=[pl.BlockSpec((ROW_BLOCK, d), lambda i: (i, 0)),
                  pl.BlockSpec((1, d), lambda i: (0, 0))],
        out_specs=pl.BlockSpec((ROW_BLOCK, d), lambda i: (i, 0)),
        out_shape=jax.ShapeDtypeStruct((seq, d), F32),
        compiler_params=_params("parallel"),
        name="final_norm",
    )(xc, g.reshape(1, d))


def _mm_tiles(t, n, wide=False):
    tm = _pick(t, (1280, 1024, 768, 640, 512, 384, 256))
    tn = _pick(n, ((1024,) if wide else ()) + (512, 384, 256, 128))
    return tm, tn


def _w_spec(k, tn, layer, col0=0):
    return pl.BlockSpec((None, k, tn), lambda i, j: (layer, 0, col0 + j))


def _up_kernel(h_ref, w1_ref, w3_ref, o_ref):
    h = h_ref[...]
    a = _dot(h, w1_ref[...])
    o_ref[...] = (_silu(a) * _dot(h, w3_ref[...])).astype(o_ref.dtype)


def swiglu_up(h, w1, w3, layer):
    t, d = h.shape
    n = w1.shape[2]
    tm, tn = _mm_tiles(t, n)
    return pl.pallas_call(
        _up_kernel,
        grid=(t // tm, n // tn),
        in_specs=[pl.BlockSpec((tm, d), lambda i, j: (i, 0)),
                  _w_spec(d, tn, layer), _w_spec(d, tn, layer)],
        out_specs=pl.BlockSpec((tm, tn), lambda i, j: (i, j)),
        out_shape=jax.ShapeDtypeStruct((t, n), BF16),
        compiler_params=_params("parallel", "arbitrary"),
        name="swiglu_up",
    )(h, w1, w3)


def _down_kernel(*refs, coef, tm, seq, n_latent_last, shadow):
    split = n_latent_last is not None
    it = iter(refs)
    a_ref, w_ref, x_ref = next(it), next(it), next(it)
    c_ref = next(it) if split else None
    gate_ref, o_ref = next(it), next(it)
    s_ref = next(it) if shadow else None
    i = pl.program_id(0)
    rows = i * tm + lax.broadcasted_iota(jnp.int32, (tm, 1), 0)
    gate = jnp.where(rows >= seq, gate_ref[1:2, :], gate_ref[0:1, :])
    upd = (coef * gate) * _dot(a_ref[...], w_ref[...])

    def emit(new, rs):
        o_ref[rs, :] = new
        if shadow:
            s_ref[rs, :] = new.astype(s_ref.dtype)

    if not split:
        emit(x_ref[...] + upd, slice(None))
        return
    last = pl.num_programs(0) - 1

    @pl.when(i < last)
    def _():
        emit(x_ref[...] + upd, slice(None))

    @pl.when(i == last)
    def _():
        lat, cx = slice(0, n_latent_last), slice(n_latent_last, tm)
        emit(x_ref[lat, :] + upd[lat, :], lat)
        emit(c_ref[...] + upd[cx, :], cx)


def down_residual(a, w, layer, res, gate, coef, seq, shadow=False):
    t, k = a.shape
    n = w.shape[2]
    tm, tn = _mm_tiles(t, n)
    split = isinstance(res, tuple)
    tile = pl.BlockSpec((tm, tn), lambda i, j: (i, j))
    if split:
        ctx_len = res[1].shape[0]
        n_latent_last = tm - ctx_len
        assert res[0].shape[0] == seq and seq + ctx_len == t and 0 < n_latent_last and n_latent_last % 16 == 0
        res_specs = [tile, pl.BlockSpec((ctx_len, tn), lambda i, j: (0, j))]
    else:
        res, n_latent_last = (res,), None
        res_specs = [tile]
    out = pl.pallas_call(
        functools.partial(_down_kernel, coef=coef, tm=tm, seq=seq, n_latent_last=n_latent_last, shadow=shadow),
        grid=(t // tm, n // tn),
        in_specs=[pl.BlockSpec((tm, k), lambda i, j: (i, 0)), _w_spec(k, tn, layer)] + res_specs
                 + [pl.BlockSpec((2, tn), lambda i, j: (0, j))],
        out_specs=[tile] * (1 + shadow),
        out_shape=[jax.ShapeDtypeStruct((t, n), F32)] + [jax.ShapeDtypeStruct((t, n), BF16)] * shadow,
        input_output_aliases={} if split else {2: 0},
        compiler_params=_params("parallel", "arbitrary"),
        name="down_residual",
    )(a, w, *res, gate)
    return out if shadow else out[0]


def _proj_kernel(h_ref, wt_ref, b_ref, o_ref, *, sigmoid):
    acc = lax.dot_general(h_ref[...], wt_ref[...], (((1,), (1,)), ((), ())), preferred_element_type=F32) + b_ref[...]
    if sigmoid:
        acc = jax.nn.sigmoid(acc)
    o_ref[...] = acc.astype(o_ref.dtype)


def proj(h, wt, b, layer, col0, n, out_dtype, sigmoid=False, name="proj"):
    t, d = h.shape
    tm, tn = _mm_tiles(t, n, wide=True)
    nl, nw, _ = wt.shape
    row0 = layer * nw + col0
    assert tn % LANES == 0 and n % tn == 0 and row0 % 16 == 0
    return pl.pallas_call(
        functools.partial(_proj_kernel, sigmoid=sigmoid),
        grid=(t // tm, n // tn),
        in_specs=[pl.BlockSpec((tm, d), lambda i, j: (i, 0)),
                  pl.BlockSpec((pl.Element(tn), pl.Element(d)), lambda i, j: (pl.multiple_of(row0 + j * tn, 16), 0)),
                  pl.BlockSpec((1, tn), lambda i, j: (0, j))],
        out_specs=pl.BlockSpec((tm, tn), lambda i, j: (i, j)),
        out_shape=jax.ShapeDtypeStruct((t, n), out_dtype),
        compiler_params=_params("parallel", "arbitrary"),
        name=name,
    )(h, wt.reshape(nl * nw, d), b[:n].reshape(1, n))


def _merge_kernel(ys_ref, ym_ref, yf_ref, ws_ref, wm_ref, wf_ref, gs_ref, gm_ref, gf_ref, o_ref):
    acc = gs_ref[...].astype(F32) * _dot(ys_ref[...], ws_ref[...])
    acc = acc + gm_ref[...].astype(F32) * _dot(ym_ref[...], wm_ref[...])
    acc = acc + gf_ref[...].astype(F32) * _dot(yf_ref[...], wf_ref[...])
    o_ref[...] = acc.astype(o_ref.dtype)


def merge(y_ssd, y_mlp, y_fft, w_ssd, w_mlp, w_fft, layer, gates):
    t = y_ssd.shape[0]
    n = w_ssd.shape[2]
    tm, tn = _mm_tiles(t, n)
    nb = n // tn
    row = lambda k: pl.BlockSpec((tm, k), lambda i, j: (i, 0))
    col = lambda k: _w_spec(k, tn, layer)
    gate = lambda b: pl.BlockSpec((tm, tn), lambda i, j: (i, b * nb + j))
    return pl.pallas_call(
        _merge_kernel,
        grid=(t // tm, nb),
        in_specs=[row(y_ssd.shape[1]), row(y_mlp.shape[1]), row(y_fft.shape[1]),
                  col(w_ssd.shape[1]), col(w_mlp.shape[1]), col(w_fft.shape[1]),
                  gate(0), gate(1), gate(2)],
        out_specs=pl.BlockSpec((tm, tn), lambda i, j: (i, j)),
        out_shape=jax.ShapeDtypeStruct((t, n), BF16),
        compiler_params=_params("parallel", "arbitrary"),
        name="merge",
    )(y_ssd, y_mlp, y_fft, w_ssd, w_mlp, w_fft, gates, gates, gates)


def _conv_kernel(u_ref, w_ref, b_ref, o_ref, *, nx, ctx_len):
    u = u_ref[...].astype(F32)
    rb = u.shape[0]
    width = jnp.where(pl.program_id(0) >= nx, ctx_len, GRID_W)
    pos = lax.broadcasted_iota(jnp.int32, (rb, 1), 0) & (width - 1)
    y = jnp.broadcast_to(b_ref[...], u.shape)
    for k in range(D_CONV):
        off = k - D_CONV // 2
        shifted = u if off == 0 else pltpu.roll(u, (-off) % rb, 0)
        valid = (pos + off >= 0) & (pos + off < width)
        y = y + jnp.where(valid, shifted, 0.0) * w_ref[k:k + 1, :]
    o_ref[...] = _silu(y).astype(o_ref.dtype)


def conv_silu(zx, col0, conv_w, conv_b, seq, ctx_len):
    t = zx.shape[0]
    c = conv_w.shape[1]
    tc = next(v for v in (1024, 512, 256, 128) if c % v == 0 and col0 % v == 0)
    assert ctx_len == ROW_BLOCK and ROW_BLOCK % GRID_W == 0
    assert GRID_W & (GRID_W - 1) == 0 and ctx_len & (ctx_len - 1) == 0
    nx = seq // ROW_BLOCK
    return pl.pallas_call(
        functools.partial(_conv_kernel, nx=nx, ctx_len=ctx_len),
        grid=(t // ROW_BLOCK, c // tc),
        in_specs=[pl.BlockSpec((ROW_BLOCK, tc), lambda i, j: (i, col0 // tc + j)),
                  pl.BlockSpec((D_CONV, tc), lambda i, j: (0, j)),
                  pl.BlockSpec((1, tc), lambda i, j: (0, j))],
        out_specs=pl.BlockSpec((ROW_BLOCK, tc), lambda i, j: (i, j)),
        out_shape=jax.ShapeDtypeStruct((t, c), BF16),
        compiler_params=_params("parallel", "arbitrary"),
        name="conv_silu",
    )(zx, conv_w, conv_b.reshape(1, c))


LOG2E = float(np.log2(np.e))


def _decay_kernel(raw_ref, bias_ref, a_ref, cs_ref, src_ref, wst_ref, etot_ref, *, heads, chunks):
    qi = lax.broadcasted_iota(jnp.int32, (CHUNK, CHUNK), 0)
    si = lax.broadcasted_iota(jnp.int32, (CHUNK, CHUNK), 1)
    lower, upper = (si <= qi).astype(F32), (si >= qi).astype(F32)
    fwd_col = lax.broadcasted_iota(jnp.int32, (CHUNK, LANES), 1) < heads
    for c in range(chunks):
        rows = slice(c * CHUNK, (c + 1) * CHUNK)
        v = raw_ref[rows, :] + bias_ref[...]
        dt = jnp.maximum(v, 0.0) + jnp.log1p(jnp.exp(-jnp.abs(v)))
        a = dt * a_ref[...]
        pre, suf = _dot(lower, a, HI), _dot(upper, a, HI)
        cs = jnp.where(fwd_col, pre, suf)
        tot = jnp.where(fwd_col[0:1, :], pre[CHUNK - 1:CHUNK, :], suf[0:1, :])
        cs_ref[rows, :] = cs * LOG2E
        src_ref[rows, :] = ((cs - jnp.log(dt)) * LOG2E).T
        wst_ref[rows, :] = (jnp.exp(tot - cs) * dt).T
        etot_ref[c * 8:(c + 1) * 8, :] = jnp.broadcast_to(jnp.exp(tot), (8, LANES))


def ssd_decays(dt_raw, dt_bias, a_neg):
    t = dt_raw.shape[0]
    nc = t // CHUNK
    heads = dt_bias.shape[0] // 2
    chunks = _pick(nc, (10, 6, 5, 4, 3, 2, 1))
    pad = lambda vec: jnp.pad(vec.reshape(1, -1), ((0, 0), (0, LANES - vec.shape[0])))
    big = pl.BlockSpec((chunks * CHUNK, LANES), lambda i: (i, 0))
    vec = pl.BlockSpec((1, LANES), lambda i: (0, 0))
    full = jax.ShapeDtypeStruct((t, LANES), F32)
    return pl.pallas_call(
        functools.partial(_decay_kernel, heads=heads, chunks=chunks),
        grid=(nc // chunks,),
        in_specs=[big, vec, vec],
        out_specs=[big, big, big, pl.BlockSpec((chunks * 8, LANES), lambda i: (i, 0))],
        out_shape=[full, full, full, jax.ShapeDtypeStruct((nc * 8, LANES), F32)],
        compiler_params=_params("parallel"),
        name="ssd_decays",
    )(dt_raw, pad(dt_bias), pad(a_neg))


def _ssd_kernel(*refs, direction, heads, d_ssd, d_state, final):
    if final:
        x_ref, cs_ref, src_ref, wst_ref, etot_ref, yf_ref, z_ref, dsk_ref, g_ref, o_ref, state_ref = refs
    else:
        x_ref, cs_ref, src_ref, wst_ref, etot_ref, o_ref, state_ref = refs
    gp = d_ssd // SSD_GROUPS
    r_heads = gp // SSD_HEAD_DIM
    gn = SSD_GROUPS * d_state

    @pl.when(pl.program_id(0) == 0)
    def _():
        state_ref[...] = jnp.zeros_like(state_ref)

    qi = lax.broadcasted_iota(jnp.int32, (CHUNK, CHUNK), 0)
    si = lax.broadcasted_iota(jnp.int32, (CHUNK, CHUNK), 1)
    causal = (si <= qi) if direction == 0 else (si >= qi)
    cs = cs_ref[...]
    src_t = src_ref[...]
    wst_t = wst_ref[...]
    etot = etot_ref[0:1, :]
    lo = lax.broadcasted_iota(jnp.int32, (CHUNK, LANES), 1) < SSD_HEAD_DIM
    lo_row = lo[0:1, :]

    for g in range(SSD_GROUPS):
        h0 = direction * heads + g * r_heads
        bm = x_ref[:, d_ssd + g * d_state:d_ssd + (g + 1) * d_state]
        cm = x_ref[:, d_ssd + gn + g * d_state:d_ssd + gn + (g + 1) * d_state]
        cb = lax.dot_general(cm, bm, (((1,), (1,)), ((), ())), preferred_element_type=F32)
        bt = bm.astype(F32).T
        y_state = _dot(cm, state_ref[:, g * gp:(g + 1) * gp].astype(BF16))
        ys = []
        for k in range(r_heads // 2):
            ps = slice(g * gp + k * LANES, g * gp + (k + 1) * LANES)
            xd = x_ref[:, ps].astype(F32)
            y_acc = st_acc = ecs = None
            for half in range(2):
                h = h0 + 2 * k + half
                csq = jnp.broadcast_to(cs[:, h:h + 1], (CHUNK, CHUNK))
                m = (cb * jnp.exp2(jnp.where(causal, csq - src_t[h:h + 1, :], -1e30))).astype(BF16)
                xh = (jnp.where(lo, xd, 0.0) if half == 0 else jnp.where(lo, 0.0, xd)).astype(BF16)
                btw = (bt * wst_t[h:h + 1, :]).astype(BF16)
                py, pst = _dot(m, xh), _dot(btw, xh)
                e = jnp.exp2(csq)
                if half == 0:
                    y_acc, st_acc, ecs = py, pst, e
                else:
                    y_acc, st_acc, ecs = y_acc + py, st_acc + pst, jnp.where(lo, ecs, e)
            ys.append(y_acc + ecs * y_state[:, k * LANES:(k + 1) * LANES])
            h = h0 + 2 * k
            decay = jnp.where(lo_row, jnp.broadcast_to(etot[:, h:h + 1], (1, LANES)),
                              jnp.broadcast_to(etot[:, h + 1:h + 2], (1, LANES)))
            state_ref[:, ps] = state_ref[:, ps] * decay + st_acc

        gs = slice(g * gp, (g + 1) * gp)
        y = ys[0] if len(ys) == 1 else jnp.concatenate(ys, axis=1)
        if final:
            y = (yf_ref[:, gs] + y + dsk_ref[:, gs] * x_ref[:, gs].astype(F32)) * _silu(z_ref[:, gs].astype(F32))
            y = y * lax.rsqrt(jnp.mean(y * y, axis=-1, keepdims=True) + EPS) * g_ref[:, gs]
        o_ref[:, gs] = y.astype(o_ref.dtype)


def ssd_mixer(xbc, zx, dt_raw, dt_bias, a_neg, d_skip_x, norm_g, seq, d_ssd, d_state):
    t, c_xbc = xbc.shape
    nc = t // CHUNK
    nxc = seq // CHUNK
    ncc = nc - nxc
    heads = dt_bias.shape[0] // 2
    gp = d_ssd // SSD_GROUPS
    assert d_state == LANES and (gp // SSD_HEAD_DIM) % 2 == 0 and gp % LANES == 0 and 2 * heads <= LANES
    fwd = lambda j: jnp.where(j < ncc, nxc + j, j - ncc)
    bwd = lambda j: nc - 1 - j
    vec = lambda n: pl.BlockSpec((1, n), lambda j: (0, 0))
    decays = ssd_decays(dt_raw, dt_bias, a_neg)

    def call(direction, order, final, extra_in, extra_specs, out_dtype):
        per_chunk = pl.BlockSpec((CHUNK, LANES), lambda j: (order(j), 0))
        return pl.pallas_call(
            functools.partial(_ssd_kernel, direction=direction, heads=heads, d_ssd=d_ssd, d_state=d_state,
                              final=final),
            grid=(nc,),
            in_specs=[pl.BlockSpec((CHUNK, c_xbc), lambda j: (order(j), 0)),
                      per_chunk, per_chunk, per_chunk,
                      pl.BlockSpec((8, LANES), lambda j: (order(j), 0))] + extra_specs,
            out_specs=pl.BlockSpec((CHUNK, d_ssd), lambda j: (order(j), 0)),
            out_shape=jax.ShapeDtypeStruct((t, d_ssd), out_dtype),
            scratch_shapes=[pltpu.VMEM((d_state, d_ssd), F32)],
            compiler_params=_params("arbitrary"),
            name="ssd_bwd_out" if final else "ssd_fwd",
        )(xbc, *decays, *extra_in)

    y_f = call(0, fwd, False, [], [], F32)
    blk = pl.BlockSpec((CHUNK, d_ssd), lambda j: (bwd(j), 0))
    return call(1, bwd, True,
                [y_f, zx, d_skip_x.reshape(1, d_ssd), norm_g.reshape(1, d_ssd)],
                [blk, blk, vec(d_ssd), vec(d_ssd)], BF16)


def _sgu_kernel(u_ref, v_ref, g_ref, w_ref, bt_ref, o_ref, *, groups):
    for c in range(u_ref.shape[0] // CHUNK):
        rows = slice(c * CHUNK, (c + 1) * CHUNK)
        u = _gelu_tanh(u_ref[rows, :].astype(F32))
        v = _gelu_tanh(v_ref[rows, :].astype(F32))
        v = (v * lax.rsqrt(jnp.mean(v * v, axis=-1, keepdims=True) + EPS) * g_ref[...]).astype(BF16)
        for g in range(groups):
            sl = slice(g * GROUP_DIM, (g + 1) * GROUP_DIM)
            vm = _dot(w_ref[g], v[:, sl]) + bt_ref[:, g:g + 1]
            o_ref[rows, sl] = (u[:, sl] * vm).astype(o_ref.dtype)


def sgu(uvf, norm_g, w_s, b_s):
    t = uvf.shape[0]
    groups = w_s.shape[0]
    dm = groups * GROUP_DIM
    rb = CHUNK * _pick(t // CHUNK, (10, 5, 2, 1))
    return pl.pallas_call(
        functools.partial(_sgu_kernel, groups=groups),
        grid=(t // rb,),
        in_specs=[pl.BlockSpec((rb, dm), lambda c: (c, 0)),
                  pl.BlockSpec((rb, dm), lambda c: (c, 1)),
                  pl.BlockSpec((1, dm), lambda c: (0, 0)),
                  pl.BlockSpec((groups, CHUNK, CHUNK), lambda c: (0, 0, 0)),
                  pl.BlockSpec((CHUNK, groups), lambda c: (0, 0))],
        out_specs=pl.BlockSpec((rb, dm), lambda c: (c, 0)),
        out_shape=jax.ShapeDtypeStruct((t, dm), BF16),
        compiler_params=_params("parallel"),
        name="sgu",
    )(uvf, uvf, norm_g.reshape(1, dm), w_s.astype(BF16), b_s.T)


def _channel_dft(f, cs_tab):
    gr, gs = [], []
    for g in range(f.shape[1] // GROUP_DIM):
        p = _dot(f[:, g * GROUP_DIM:(g + 1) * GROUP_DIM], cs_tab)
        gr.append(p[:, :GROUP_DIM])
        gs.append(p[:, GROUP_DIM:])
    cat = lambda v: (v[0] if len(v) == 1 else jnp.concatenate(v, axis=1)).astype(BF16)
    return cat(gr), cat(gs)


def _fft_a_kernel(f_ref, m_ref, cs_ref, z_ref, *, n1):
    for a in range(f_ref.shape[0]):
        gr, gs = _channel_dft(f_ref[a], cs_ref[...])
        m = m_ref[a]
        z_ref[a] = (_dot(m[:, :n1], gr) + _dot(m[:, n1:], gs)).astype(z_ref.dtype)


def _fft_b_kernel(zr_ref, zi_ref, c_ref, s_ref, o_ref, *, scale):
    acc = _dot(c_ref[...], zr_ref[...]) - _dot(s_ref[...], zi_ref[...])
    o_ref[...] = (acc * scale).astype(o_ref.dtype)


def _fft_ctx_kernel(f_ref, c_ref, s_ref, cs_ref, o_ref, *, scale):
    gr, gs = _channel_dft(f_ref[...], cs_ref[...])
    acc = _dot(c_ref[...], gr) - _dot(s_ref[...], gs)
    o_ref[...] = (acc * scale).astype(o_ref.dtype)


def _cos_sin(n):
    ang = 2.0 * np.pi * (np.outer(np.arange(n), np.arange(n)) % n) / n
    return np.cos(ang), np.sin(ang)


def fourier_mix(uvf, col0, d_fft, seq, ctx_len):
    n2 = CHUNK
    n1 = seq // n2
    assert n1 * n2 == seq and n1 % 16 == 0 and d_fft % GROUP_DIM == 0
    cc, sc = _cos_sin(GROUP_DIM)
    cs_tab = jnp.asarray(np.concatenate([cc, sc], axis=1), BF16)
    c1, s1 = _cos_sin(n1)
    ang = 2.0 * np.pi * np.outer(np.arange(n2), np.arange(n1)) / seq
    ct, st = (jnp.asarray(v[:, :, None], F32) for v in (np.cos(ang), np.sin(ang)))
    c1, s1 = jnp.asarray(c1[None], F32), jnp.asarray(s1[None], F32)
    mc = c1 * ct - s1 * st
    ms = s1 * ct + c1 * st
    m_tab = jnp.concatenate([jnp.concatenate([mc, -ms], axis=2),
                             jnp.concatenate([ms, mc], axis=2)], axis=1).astype(BF16)
    c2, s2 = (jnp.asarray(v, BF16) for v in _cos_sin(n2))

    f_t = uvf[:seq, col0:col0 + d_fft].reshape(n1, n2, d_fft).transpose(1, 0, 2)
    ta = _pick(n2, (8, 4, 2, 1))
    z = pl.pallas_call(
        functools.partial(_fft_a_kernel, n1=n1),
        grid=(n2 // ta,),
        in_specs=[pl.BlockSpec((ta, n1, d_fft), lambda a: (a, 0, 0)),
                  pl.BlockSpec((ta, 2 * n1, 2 * n1), lambda a: (a, 0, 0)),
                  pl.BlockSpec((GROUP_DIM, 2 * GROUP_DIM), lambda a: (0, 0))],
        out_specs=pl.BlockSpec((ta, 2 * n1, d_fft), lambda a: (a, 0, 0)),
        out_shape=jax.ShapeDtypeStruct((n2, 2 * n1, d_fft), BF16),
        compiler_params=_params("parallel"),
        name="fft_stage_a",
    )(f_t, m_tab, cs_tab)

    width = n1 * d_fft
    tw = _pick(width, (8192, 4096, 2048, 1024, 512, 256, 128))
    nb = width // tw
    z2 = z.reshape(n2, 2 * width)
    tab = lambda: pl.BlockSpec((n2, n2), lambda j: (0, 0))
    y_x = pl.pallas_call(
        functools.partial(_fft_b_kernel, scale=float(1.0 / np.sqrt(seq * GROUP_DIM))),
        grid=(nb,),
        in_specs=[pl.BlockSpec((n2, tw), lambda j: (0, j)),
                  pl.BlockSpec((n2, tw), lambda j: (0, nb + j)),
                  tab(), tab()],
        out_specs=pl.BlockSpec((n2, tw), lambda j: (0, j)),
        out_shape=jax.ShapeDtypeStruct((n2, width), BF16),
        compiler_params=_params("parallel"),
        name="fft_stage_b",
    )(z2, z2, c2, s2).reshape(seq, d_fft)

    assert ctx_len == ROW_BLOCK and col0 % d_fft == 0
    cl, sl = (jnp.asarray(v, BF16) for v in _cos_sin(ctx_len))
    ctab = lambda: pl.BlockSpec((ctx_len, ctx_len), lambda j: (0, 0))
    y_c = pl.pallas_call(
        functools.partial(_fft_ctx_kernel, scale=float(1.0 / np.sqrt(ctx_len * GROUP_DIM))),
        grid=(1,),
        in_specs=[pl.BlockSpec((ctx_len, d_fft), lambda j: (seq // ctx_len, col0 // d_fft)),
                  ctab(), ctab(),
                  pl.BlockSpec((GROUP_DIM, 2 * GROUP_DIM), lambda j: (0, 0))],
        out_specs=pl.BlockSpec((ctx_len, d_fft), lambda j: (0, 0)),
        out_shape=jax.ShapeDtypeStruct((ctx_len, d_fft), BF16),
        compiler_params=_params("arbitrary"),
        name="fft_ctx",
    )(uvf, cl, sl, cs_tab)
    return jnp.concatenate([y_x, y_c], axis=0)


def kernel(x, c, ctx, c_ctx, ada_w1, ada_w2, ada_b, norm_ffn1, f1_w1, f1_w3, f1_w2, norm_mix, w_in, b_in,
           conv_w, conv_b, a_log, dt_bias, d_skip, norm_ssd, sgu_norm, sgu_w, sgu_b,
           w_br_ssd, w_br_mlp, w_br_fft, w_out, norm_ffn2, f2_w1, f2_w3, f2_w2, norm_final):
    batch, seq, d = x.shape
    ctx_len = ctx.shape[1]
    depth = ada_w1.shape[0]
    assert batch == 1 and seq % ROW_BLOCK == 0
    heads = a_log.shape[-1]
    d_ssd = norm_ssd.shape[-1]
    c_xbc = conv_w.shape[-1]
    d_state = (c_xbc - d_ssd) // (2 * SSD_GROUPS)
    d_mlp = sgu_norm.shape[-1]
    d_fft = w_br_fft.shape[1]
    assert heads * SSD_HEAD_DIM == d_ssd and d_mlp == d_fft
    off_xbc = d_ssd
    off_dt = off_xbc + c_xbc
    off_u = off_dt + 2 * heads
    off_gate = off_u + 2 * d_mlp + d_fft

    cv = jnp.zeros((8, d), F32).at[0].set(c[0]).at[1].set(c_ctx)
    mods = ada_mods(cv, ada_w1, ada_w2, ada_b)[:, 0:2, :].reshape(depth, 2, N_MOD, d)
    bf = lambda w: w.astype(BF16)
    f1_w1, f1_w3, f1_w2, f2_w1, f2_w3, f2_w2 = map(bf, (f1_w1, f1_w3, f1_w2, f2_w1, f2_w3, f2_w2))
    w_br_ssd, w_br_mlp, w_br_fft, w_out = map(bf, (w_br_ssd, w_br_mlp, w_br_fft, w_out))
    w_in_t = bf(jnp.swapaxes(w_in, 1, 2))
    b_dt = jnp.pad(b_in[:, off_dt:off_u], ((0, 0), (0, LANES - 2 * heads)))
    n_uvf = off_gate - off_u

    xc = xb = (x[0], ctx[0])
    for i in range(depth):
        sh1, sc1, g1, shm, scm, gm, sh2, sc2, g2 = (mods[i, :, k, :] for k in range(N_MOD))

        h = norm_mod(xb, norm_ffn1[i], sh1, sc1, seq)
        xc, xb = down_residual(swiglu_up(h, f1_w1, f1_w3, i), f1_w2, i, xc, g1, 0.5, seq, shadow=True)

        h = norm_mod(xb, norm_mix[i], shm, scm, seq)
        zx = proj(h, w_in_t, b_in[i], i, 0, off_dt, BF16, name="proj_zx")
        dt_raw = proj(h, w_in_t, b_dt[i], i, off_dt, LANES, F32, name="proj_dt")
        uvf = proj(h, w_in_t, b_in[i, off_u:], i, off_u, n_uvf, BF16, name="proj_uvf")
        gates = proj(h, w_in_t, b_in[i, off_gate:], i, off_gate, 3 * d, BF16, sigmoid=True, name="proj_gates")

        xbc = conv_silu(zx, off_xbc, conv_w[i], conv_b[i], seq, ctx_len)
        a_neg = -jnp.exp(a_log[i].astype(F32)).reshape(-1)
        y_ssd = ssd_mixer(xbc, zx, dt_raw, dt_bias[i].reshape(-1), a_neg,
                          jnp.repeat(d_skip[i], SSD_HEAD_DIM), norm_ssd[i], seq, d_ssd, d_state)
        y_mlp = sgu(uvf, sgu_norm[i], sgu_w[i], sgu_b[i])
        y_fft = fourier_mix(uvf, 2 * d_mlp, d_fft, seq, ctx_len)

        merged = merge(y_ssd, y_mlp, y_fft, w_br_ssd, w_br_mlp, w_br_fft, i, gates)
        xc, xb = down_residual(merged, w_out, i, xc, gm, 1.0, seq, shadow=True)

        h = norm_mod(xb, norm_ffn2[i], sh2, sc2, seq)
        up = swiglu_up(h, f2_w1, f2_w3, i)
        if i + 1 < depth:
            xc, xb = down_residual(up, f2_w2, i, xc, g2, 0.5, seq, shadow=True)
        else:
            xc = down_residual(up, f2_w2, i, xc, g2, 0.5, seq)

    return final_norm(xc, norm_final, seq).reshape(1, seq, d)
```

```python
import functools

import numpy as np
import jax
import jax.numpy as jnp
from jax import lax
from jax.experimental import pallas as pl
from jax.experimental.pallas import tpu as pltpu

F32 = jnp.float32
BF16 = jnp.bfloat16
HI = lax.Precision.HIGHEST

EPS = 1e-6
GRID_W = 64
SSD_GROUPS = 4
SSD_HEAD_DIM = 64
CHUNK = 128
D_CONV = 5
GROUP_DIM = 128
N_MOD = 9
LANES = 128
ROW_BLOCK = 256
VMEM_LIMIT = 56 * 1024 * 1024


def _pick(n, cands):
    for c in cands:
        if n % c == 0:
            return c
    raise ValueError(f"no tile for {n} in {cands}")


def _params(*sem):
    return pltpu.CompilerParams(dimension_semantics=sem, vmem_limit_bytes=VMEM_LIMIT)


def _silu(v):
    return v * jax.nn.sigmoid(v)


def _gelu_tanh(v):
    return v * (0.5 * (1.0 + jnp.tanh(np.sqrt(2.0 / np.pi).astype(np.float32) * (v + 0.044715 * (v * v * v)))))


def _dot(a, b, precision=None):
    return jnp.dot(a, b, preferred_element_type=F32, precision=precision)


def _ada_kernel(cv_ref, w1_ref, w2_ref, b_ref, o_ref):
    t = _dot(_silu(cv_ref[...]), w1_ref[0], HI)
    o_ref[0] = _dot(t, w2_ref[0], HI) + b_ref[0]


def ada_mods(cv, w1, w2, b):
    nl, d, r = w1.shape
    n = w2.shape[-1]
    tn = _pick(n, (4608, 4096, 2304, 2048, 1024, 512, 256, 128))
    return pl.pallas_call(
        _ada_kernel,
        grid=(nl, n // tn),
        in_specs=[pl.BlockSpec((8, d), lambda l, j: (0, 0)),
                  pl.BlockSpec((1, d, r), lambda l, j: (l, 0, 0)),
                  pl.BlockSpec((1, r, tn), lambda l, j: (l, 0, j)),
                  pl.BlockSpec((1, 1, tn), lambda l, j: (l, 0, j))],
        out_specs=pl.BlockSpec((1, 8, tn), lambda l, j: (l, 0, j)),
        out_shape=jax.ShapeDtypeStruct((nl, 8, n), F32),
        compiler_params=_params("arbitrary", "arbitrary"),
        name="ada_mods",
    )(cv, w1, w2, b.reshape(nl, 1, n))


def _norm_mod_kernel(*refs, nx, split):
    if split:
        x_ref, c_ref, gm_ref, sh_ref, o_ref = refs
    else:
        x_ref, gm_ref, sh_ref, o_ref = refs

    def emit(src_ref):
        x = src_ref[...].astype(F32)
        rstd = lax.rsqrt(jnp.mean(x * x, axis=-1, keepdims=True) + EPS)
        o_ref[...] = (x * rstd * gm_ref[0] + sh_ref[0]).astype(o_ref.dtype)

    if split:
        pl.when(pl.program_id(0) < nx)(lambda: emit(x_ref))
        pl.when(pl.program_id(0) >= nx)(lambda: emit(c_ref))
    else:
        emit(x_ref)


def norm_mod(xs, g, shift, scale, seq):
    split = isinstance(xs, tuple)
    d = g.shape[0]
    nx = seq // ROW_BLOCK
    sel = lambda i: (jnp.where(i >= nx, 1, 0), 0, 0)
    gmul = g[None, :] * (1.0 + scale)
    if split:
        t = seq + xs[1].shape[0]
        src_specs = [pl.BlockSpec((ROW_BLOCK, d), lambda i: (jnp.minimum(i, nx - 1), 0)),
                     pl.BlockSpec((ROW_BLOCK, d), lambda i: (jnp.maximum(i - nx, 0), 0))]
    else:
        xs = (xs,)
        t = xs[0].shape[0]
        src_specs = [pl.BlockSpec((ROW_BLOCK, d), lambda i: (i, 0))]
    return pl.pallas_call(
        functools.partial(_norm_mod_kernel, nx=nx, split=split),
        grid=(t // ROW_BLOCK,),
        in_specs=src_specs + [pl.BlockSpec((1, 1, d), sel), pl.BlockSpec((1, 1, d), sel)],
        out_specs=pl.BlockSpec((ROW_BLOCK, d), lambda i: (i, 0)),
        out_shape=jax.ShapeDtypeStruct((t, d), BF16),
        compiler_params=_params("parallel"),
        name="norm_mod",
    )(*xs, gmul.reshape(2, 1, d), shift.reshape(2, 1, d))


def _final_norm_kernel(x_ref, g_ref, o_ref):
    x = x_ref[...]
    o_ref[...] = x * lax.rsqrt(jnp.mean(x * x, axis=-1, keepdims=True) + EPS) * g_ref[...]


def final_norm(xc, g, seq):
    d = xc.shape[1]
    return pl.pallas_call(
        _final_norm_kernel,
        grid=(seq // ROW_BLOCK,),
        in_specs=[pl.BlockSpec((ROW_BLOCK, d), lambda i: (i, 0)),
                  pl.BlockSpec((1, d), lambda i: (0, 0))],
        out_specs=pl.BlockSpec((ROW_BLOCK, d), lambda i: (i, 0)),
        out_shape=jax.ShapeDtypeStruct((seq, d), F32),
        compiler_params=_params("parallel"),
        name="final_norm",
    )(xc, g.reshape(1, d))


def _mm_tiles(t, n, wide=False):
    tm = _pick(t, (1280, 1024, 768, 640, 512, 384, 256))
    tn = _pick(n, ((1024,) if wide else ()) + (512, 384, 256, 128))
    return tm, tn


def _w_spec(k, tn, layer, col0=0):
    return pl.BlockSpec((None, k, tn), lambda i, j: (layer, 0, col0 + j))


def _up_kernel(h_ref, w1_ref, w3_ref, o_ref):
    h = h_ref[...]
    a = _dot(h, w1_ref[...])
    o_ref[...] = (_silu(a) * _dot(h, w3_ref[...])).astype(o_ref.dtype)


def swiglu_up(h, w1, w3, layer):
    t, d = h.shape
    n = w1.shape[2]
    tm, tn = _mm_tiles(t, n)
    return pl.pallas_call(
        _up_kernel,
        grid=(t // tm, n // tn),
        in_specs=[pl.BlockSpec((tm, d), lambda i, j: (i, 0)),
                  _w_spec(d, tn, layer), _w_spec(d, tn, layer)],
        out_specs=pl.BlockSpec((tm, tn), lambda i, j: (i, j)),
        out_shape=jax.ShapeDtypeStruct((t, n), BF16),
        compiler_params=_params("parallel", "arbitrary"),
        name="swiglu_up",
    )(h, w1, w3)


def _down_kernel(*refs, coef, tm, seq, n_latent_last, shadow):
    split = n_latent_last is not None
    it = iter(refs)
    a_ref, w_ref, x_ref = next(it), next(it), next(it)
    c_ref = next(it) if split else None
    gate_ref, o_ref = next(it), next(it)
    s_ref = next(it) if shadow else None
    i = pl.program_id(0)
    rows = i * tm + lax.broadcasted_iota(jnp.int32, (tm, 1), 0)
    gate = jnp.where(rows >= seq, gate_ref[1:2, :], gate_ref[0:1, :])
    upd = (coef * gate) * _dot(a_ref[...], w_ref[...])

    def emit(new, rs):
        o_ref[rs, :] = new
        if shadow:
            s_ref[rs, :] = new.astype(s_ref.dtype)

    if not split:
        emit(x_ref[...] + upd, slice(None))
        return
    last = pl.num_programs(0) - 1

    @pl.when(i < last)
    def _():
        emit(x_ref[...] + upd, slice(None))

    @pl.when(i == last)
    def _():
        lat, cx = slice(0, n_latent_last), slice(n_latent_last, tm)
        emit(x_ref[lat, :] + upd[lat, :], lat)
        emit(c_ref[...] + upd[cx, :], cx)


def down_residual(a, w, layer, res, gate, coef, seq, shadow=False):
    t, k = a.shape
    n = w.shape[2]
    tm, tn = _mm_tiles(t, n)
    split = isinstance(res, tuple)
    tile = pl.BlockSpec((tm, tn), lambda i, j: (i, j))
    if split:
        ctx_len = res[1].shape[0]
        n_latent_last = tm - ctx_len
        assert res[0].shape[0] == seq and seq + ctx_len == t and 0 < n_latent_last and n_latent_last % 16 == 0
        res_specs = [tile, pl.BlockSpec((ctx_len, tn), lambda i, j: (0, j))]
    else:
        res, n_latent_last = (res,), None
        res_specs = [tile]
    out = pl.pallas_call(
        functools.partial(_down_kernel, coef=coef, tm=tm, seq=seq, n_latent_last=n_latent_last, shadow=shadow),
        grid=(t // tm, n // tn),
        in_specs=[pl.BlockSpec((tm, k), lambda i, j: (i, 0)), _w_spec(k, tn, layer)] + res_specs
                 + [pl.BlockSpec((2, tn), lambda i, j: (0, j))],
        out_specs=[tile] * (1 + shadow),
        out_shape=[jax.ShapeDtypeStruct((t, n), F32)] + [jax.ShapeDtypeStruct((t, n), BF16)] * shadow,
        input_output_aliases={} if split else {2: 0},
        compiler_params=_params("parallel", "arbitrary"),
        name="down_residual",
    )(a, w, *res, gate)
    return out if shadow else out[0]


def _proj_kernel(h_ref, wt_ref, b_ref, o_ref, *, sigmoid):
    acc = lax.dot_general(h_ref[...], wt_ref[...], (((1,), (1,)), ((), ())), preferred_element_type=F32) + b_ref[...]
    if sigmoid:
        acc = jax.nn.sigmoid(acc)
    o_ref[...] = acc.astype(o_ref.dtype)


def proj(h, wt, b, layer, col0, n, out_dtype, sigmoid=False, name="proj"):
    t, d = h.shape
    tm, tn = _mm_tiles(t, n, wide=True)
    nl, nw, _ = wt.shape
    row0 = layer * nw + col0
    assert tn % LANES == 0 and n % tn == 0 and row0 % 16 == 0
    return pl.pallas_call(
        functools.partial(_proj_kernel, sigmoid=sigmoid),
        grid=(t // tm, n // tn),
        in_specs=[pl.BlockSpec((tm, d), lambda i, j: (i, 0)),
                  pl.BlockSpec((pl.Element(tn), pl.Element(d)), lambda i, j: (pl.multiple_of(row0 + j * tn, 16), 0)),
                  pl.BlockSpec((1, tn), lambda i, j: (0, j))],
        out_specs=pl.BlockSpec((tm, tn), lambda i, j: (i, j)),
        out_shape=jax.ShapeDtypeStruct((t, n), out_dtype),
        compiler_params=_params("parallel", "arbitrary"),
        name=name,
    )(h, wt.reshape(nl * nw, d), b[:n].reshape(1, n))


def _merge_kernel(ys_ref, ym_ref, yf_ref, ws_ref, wm_ref, wf_ref, gs_ref, gm_ref, gf_ref, o_ref):
    acc = gs_ref[...].astype(F32) * _dot(ys_ref[...], ws_ref[...])
    acc = acc + gm_ref[...].astype(F32) * _dot(ym_ref[...], wm_ref[...])
    acc = acc + gf_ref[...].astype(F32) * _dot(yf_ref[...], wf_ref[...])
    o_ref[...] = acc.astype(o_ref.dtype)


def merge(y_ssd, y_mlp, y_fft, w_ssd, w_mlp, w_fft, layer, gates):
    t = y_ssd.shape[0]
    n = w_ssd.shape[2]
    tm, tn = _mm_tiles(t, n)
    nb = n // tn
    row = lambda k: pl.BlockSpec((tm, k), lambda i, j: (i, 0))
    col = lambda k: _w_spec(k, tn, layer)
    gate = lambda b: pl.BlockSpec((tm, tn), lambda i, j: (i, b * nb + j))
    return pl.pallas_call(
        _merge_kernel,
        grid=(t // tm, nb),
        in_specs=[row(y_ssd.shape[1]), row(y_mlp.shape[1]), row(y_fft.shape[1]),
                  col(w_ssd.shape[1]), col(w_mlp.shape[1]), col(w_fft.shape[1]),
                  gate(0), gate(1), gate(2)],
        out_specs=pl.BlockSpec((tm, tn), lambda i, j: (i, j)),
        out_shape=jax.ShapeDtypeStruct((t, n), BF16),
        compiler_params=_params("parallel", "arbitrary"),
        name="merge",
    )(y_ssd, y_mlp, y_fft, w_ssd, w_mlp, w_fft, gates, gates, gates)


def _conv_kernel(u_ref, w_ref, b_ref, o_ref, *, nx, ctx_len):
    u = u_ref[...].astype(F32)
    rb = u.shape[0]
    width = jnp.where(pl.program_id(0) >= nx, ctx_len, GRID_W)
    pos = lax.broadcasted_iota(jnp.int32, (rb, 1), 0) & (width - 1)
    y = jnp.broadcast_to(b_ref[...], u.shape)
    for k in range(D_CONV):
        off = k - D_CONV // 2
        shifted = u if off == 0 else pltpu.roll(u, (-off) % rb, 0)
        valid = (pos + off >= 0) & (pos + off < width)
        y = y + jnp.where(valid, shifted, 0.0) * w_ref[k:k + 1, :]
    o_ref[...] = _silu(y).astype(o_ref.dtype)


def conv_silu(zx, col0, conv_w, conv_b, seq, ctx_len):
    t = zx.shape[0]
    c = conv_w.shape[1]
    tc = next(v for v in (1024, 512, 256, 128) if c % v == 0 and col0 % v == 0)
    assert ctx_len == ROW_BLOCK and ROW_BLOCK % GRID_W == 0
    assert GRID_W & (GRID_W - 1) == 0 and ctx_len & (ctx_len - 1) == 0
    nx = seq // ROW_BLOCK
    return pl.pallas_call(
        functools.partial(_conv_kernel, nx=nx, ctx_len=ctx_len),
        grid=(t // ROW_BLOCK, c // tc),
        in_specs=[pl.BlockSpec((ROW_BLOCK, tc), lambda i, j: (i, col0 // tc + j)),
                  pl.BlockSpec((D_CONV, tc), lambda i, j: (0, j)),
                  pl.BlockSpec((1, tc), lambda i, j: (0, j))],
        out_specs=pl.BlockSpec((ROW_BLOCK, tc), lambda i, j: (i, j)),
        out_shape=jax.ShapeDtypeStruct((t, c), BF16),
        compiler_params=_params("parallel", "arbitrary"),
        name="conv_silu",
    )(zx, conv_w, conv_b.reshape(1, c))


LOG2E = float(np.log2(np.e))


def _decay_kernel(raw_ref, bias_ref, a_ref, cs_ref, src_ref, wst_ref, etot_ref, *, heads, chunks):
    qi = lax.broadcasted_iota(jnp.int32, (CHUNK, CHUNK), 0)
    si = lax.broadcasted_iota(jnp.int32, (CHUNK, CHUNK), 1)
    lower, upper = (si <= qi).astype(F32), (si >= qi).astype(F32)
    fwd_col = lax.broadcasted_iota(jnp.int32, (CHUNK, LANES), 1) < heads
    for c in range(chunks):
        rows = slice(c * CHUNK, (c + 1) * CHUNK)
        v = raw_ref[rows, :] + bias_ref[...]
        dt = jnp.maximum(v, 0.0) + jnp.log1p(jnp.exp(-jnp.abs(v)))
        a = dt * a_ref[...]
        pre, suf = _dot(lower, a, HI), _dot(upper, a, HI)
        cs = jnp.where(fwd_col, pre, suf)
        tot = jnp.where(fwd_col[0:1, :], pre[CHUNK - 1:CHUNK, :], suf[0:1, :])
        cs_ref[rows, :] = cs * LOG2E
        src_ref[rows, :] = ((cs - jnp.log(dt)) * LOG2E).T
        wst_ref[rows, :] = (jnp.exp(tot - cs) * dt).T
        etot_ref[c * 8:(c + 1) * 8, :] = jnp.broadcast_to(jnp.exp(tot), (8, LANES))


def ssd_decays(dt_raw, dt_bias, a_neg):
    t = dt_raw.shape[0]
    nc = t // CHUNK
    heads = dt_bias.shape[0] // 2
    chunks = _pick(nc, (10, 6, 5, 4, 3, 2, 1))
    pad = lambda vec: jnp.pad(vec.reshape(1, -1), ((0, 0), (0, LANES - vec.shape[0])))
    big = pl.BlockSpec((chunks * CHUNK, LANES), lambda i: (i, 0))
    vec = pl.BlockSpec((1, LANES), lambda i: (0, 0))
    full = jax.ShapeDtypeStruct((t, LANES), F32)
    return pl.pallas_call(
        functools.partial(_decay_kernel, heads=heads, chunks=chunks),
        grid=(nc // chunks,),
        in_specs=[big, vec, vec],
        out_specs=[big, big, big, pl.BlockSpec((chunks * 8, LANES), lambda i: (i, 0))],
        out_shape=[full, full, full, jax.ShapeDtypeStruct((nc * 8, LANES), F32)],
        compiler_params=_params("parallel"),
        name="ssd_decays",
    )(dt_raw, pad(dt_bias), pad(a_neg))


def _ssd_kernel(*refs, direction, heads, d_ssd, d_state, final):
    if final:
        x_ref, cs_ref, src_ref, wst_ref, etot_ref, yf_ref, z_ref, dsk_ref, g_ref, o_ref, state_ref = refs
    else:
        x_ref, cs_ref, src_ref, wst_ref, etot_ref, o_ref, state_ref = refs
    gp = d_ssd // SSD_GROUPS
    r_heads = gp // SSD_HEAD_DIM
    gn = SSD_GROUPS * d_state

    @pl.when(pl.program_id(0) == 0)
    def _():
        state_ref[...] = jnp.zeros_like(state_ref)

    qi = lax.broadcasted_iota(jnp.int32, (CHUNK, CHUNK), 0)
    si = lax.broadcasted_iota(jnp.int32, (CHUNK, CHUNK), 1)
    causal = (si <= qi) if direction == 0 else (si >= qi)
    cs = cs_ref[...]
    src_t = src_ref[...]
    wst_t = wst_ref[...]
    etot = etot_ref[0:1, :]
    lo = lax.broadcasted_iota(jnp.int32, (CHUNK, LANES), 1) < SSD_HEAD_DIM
    lo_row = lo[0:1, :]

    for g in range(SSD_GROUPS):
        h0 = direction * heads + g * r_heads
        bm = x_ref[:, d_ssd + g * d_state:d_ssd + (g + 1) * d_state]
        cm = x_ref[:, d_ssd + gn + g * d_state:d_ssd + gn + (g + 1) * d_state]
        cb = lax.dot_general(cm, bm, (((1,), (1,)), ((), ())), preferred_element_type=F32)
        bt = bm.astype(F32).T
        y_state = _dot(cm, state_ref[:, g * gp:(g + 1) * gp].astype(BF16))
        ys = []
        for k in range(r_heads // 2):
            ps = slice(g * gp + k * LANES, g * gp + (k + 1) * LANES)
            xd = x_ref[:, ps].astype(F32)
            y_acc = st_acc = ecs = None
            for half in range(2):
                h = h0 + 2 * k + half
                csq = jnp.broadcast_to(cs[:, h:h + 1], (CHUNK, CHUNK))
                m = (cb * jnp.exp2(jnp.where(causal, csq - src_t[h:h + 1, :], -1e30))).astype(BF16)
                xh = (jnp.where(lo, xd, 0.0) if half == 0 else jnp.where(lo, 0.0, xd)).astype(BF16)
                btw = (bt * wst_t[h:h + 1, :]).astype(BF16)
                py, pst = _dot(m, xh), _dot(btw, xh)
                e = jnp.exp2(csq)
                if half == 0:
                    y_acc, st_acc, ecs = py, pst, e
                else:
                    y_acc, st_acc, ecs = y_acc + py, st_acc + pst, jnp.where(lo, ecs, e)
            ys.append(y_acc + ecs * y_state[:, k * LANES:(k + 1) * LANES])
            h = h0 + 2 * k
            decay = jnp.where(lo_row, jnp.broadcast_to(etot[:, h:h + 1], (1, LANES)),
                              jnp.broadcast_to(etot[:, h + 1:h + 2], (1, LANES)))
            state_ref[:, ps] = state_ref[:, ps] * decay + st_acc

        gs = slice(g * gp, (g + 1) * gp)
        y = ys[0] if len(ys) == 1 else jnp.concatenate(ys, axis=1)
        if final:
            y = (yf_ref[:, gs] + y + dsk_ref[:, gs] * x_ref[:, gs].astype(F32)) * _silu(z_ref[:, gs].astype(F32))
            y = y * lax.rsqrt(jnp.mean(y * y, axis=-1, keepdims=True) + EPS) * g_ref[:, gs]
        o_ref[:, gs] = y.astype(o_ref.dtype)


def ssd_mixer(xbc, zx, dt_raw, dt_bias, a_neg, d_skip_x, norm_g, seq, d_ssd, d_state):
    t, c_xbc = xbc.shape
    nc = t // CHUNK
    nxc = seq // CHUNK
    ncc = nc - nxc
    heads = dt_bias.shape[0] // 2
    gp = d_ssd // SSD_GROUPS
    assert d_state == LANES and (gp // SSD_HEAD_DIM) % 2 == 0 and gp % LANES == 0 and 2 * heads <= LANES
    fwd = lambda j: jnp.where(j < ncc, nxc + j, j - ncc)
    bwd = lambda j: nc - 1 - j
    vec = lambda n: pl.BlockSpec((1, n), lambda j: (0, 0))
    decays = ssd_decays(dt_raw, dt_bias, a_neg)

    def call(direction, order, final, extra_in, extra_specs, out_dtype):
        per_chunk = pl.BlockSpec((CHUNK, LANES), lambda j: (order(j), 0))
        return pl.pallas_call(
            functools.partial(_ssd_kernel, direction=direction, heads=heads, d_ssd=d_ssd, d_state=d_state,
                              final=final),
            grid=(nc,),
            in_specs=[pl.BlockSpec((CHUNK, c_xbc), lambda j: (order(j), 0)),
                      per_chunk, per_chunk, per_chunk,
                      pl.BlockSpec((8, LANES), lambda j: (order(j), 0))] + extra_specs,
            out_specs=pl.BlockSpec((CHUNK, d_ssd), lambda j: (order(j), 0)),
            out_shape=jax.ShapeDtypeStruct((t, d_ssd), out_dtype),
            scratch_shapes=[pltpu.VMEM((d_state, d_ssd), F32)],
            compiler_params=_params("arbitrary"),
            name="ssd_bwd_out" if final else "ssd_fwd",
        )(xbc, *decays, *extra_in)

    y_f = call(0, fwd, False, [], [], F32)
    blk = pl.BlockSpec((CHUNK, d_ssd), lambda j: (bwd(j), 0))
    return call(1, bwd, True,
                [y_f, zx, d_skip_x.reshape(1, d_ssd), norm_g.reshape(1, d_ssd)],
                [blk, blk, vec(d_ssd), vec(d_ssd)], BF16)


def _sgu_kernel(u_ref, v_ref, g_ref, w_ref, bt_ref, o_ref, *, groups):
    for c in range(u_ref.shape[0] // CHUNK):
        rows = slice(c * CHUNK, (c + 1) * CHUNK)
        u = _gelu_tanh(u_ref[rows, :].astype(F32))
        v = _gelu_tanh(v_ref[rows, :].astype(F32))
        v = (v * lax.rsqrt(jnp.mean(v * v, axis=-1, keepdims=True) + EPS) * g_ref[...]).astype(BF16)
        for g in range(groups):
            sl = slice(g * GROUP_DIM, (g + 1) * GROUP_DIM)
            vm = _dot(w_ref[g], v[:, sl]) + bt_ref[:, g:g + 1]
            o_ref[rows, sl] = (u[:, sl] * vm).astype(o_ref.dtype)


def sgu(uvf, norm_g, w_s, b_s):
    t = uvf.shape[0]
    groups = w_s.shape[0]
    dm = groups * GROUP_DIM
    rb = CHUNK * _pick(t // CHUNK, (10, 5, 2, 1))
    return pl.pallas_call(
        functools.partial(_sgu_kernel, groups=groups),
        grid=(t // rb,),
        in_specs=[pl.BlockSpec((rb, dm), lambda c: (c, 0)),
                  pl.BlockSpec((rb, dm), lambda c: (c, 1)),
                  pl.BlockSpec((1, dm), lambda c: (0, 0)),
                  pl.BlockSpec((groups, CHUNK, CHUNK), lambda c: (0, 0, 0)),
                  pl.BlockSpec((CHUNK, groups), lambda c: (0, 0))],
        out_specs=pl.BlockSpec((rb, dm), lambda c: (c, 0)),
        out_shape=jax.ShapeDtypeStruct((t, dm), BF16),
        compiler_params=_params("parallel"),
        name="sgu",
    )(uvf, uvf, norm_g.reshape(1, dm), w_s.astype(BF16), b_s.T)


def _channel_dft(f, cs_tab):
    gr, gs = [], []
    for g in range(f.shape[1] // GROUP_DIM):
        p = _dot(f[:, g * GROUP_DIM:(g + 1) * GROUP_DIM], cs_tab)
        gr.append(p[:, :GROUP_DIM])
        gs.append(p[:, GROUP_DIM:])
    cat = lambda v: (v[0] if len(v) == 1 else jnp.concatenate(v, axis=1)).astype(BF16)
    return cat(gr), cat(gs)


def _fft_a_kernel(f_ref, m_ref, cs_ref, z_ref, *, n1):
    for a in range(f_ref.shape[0]):
        gr, gs = _channel_dft(f_ref[a], cs_ref[...])
        m = m_ref[a]
        z_ref[a] = (_dot(m[:, :n1], gr) + _dot(m[:, n1:], gs)).astype(z_ref.dtype)


def _fft_b_kernel(zr_ref, zi_ref, c_ref, s_ref, o_ref, *, scale):
    acc = _dot(c_ref[...], zr_ref[...]) - _dot(s_ref[...], zi_ref[...])
    o_ref[...] = (acc * scale).astype(o_ref.dtype)


def _fft_ctx_kernel(f_ref, c_ref, s_ref, cs_ref, o_ref, *, scale):
    gr, gs = _channel_dft(f_ref[...], cs_ref[...])
    acc = _dot(c_ref[...], gr) - _dot(s_ref[...], gs)
    o_ref[...] = (acc * scale).astype(o_ref.dtype)


def _cos_sin(n):
    ang = 2.0 * np.pi * (np.outer(np.arange(n), np.arange(n)) % n) / n
    return np.cos(ang), np.sin(ang)


def fourier_mix(uvf, col0, d_fft, seq, ctx_len):
    n2 = CHUNK
    n1 = seq // n2
    assert n1 * n2 == seq and n1 % 16 == 0 and d_fft % GROUP_DIM == 0
    cc, sc = _cos_sin(GROUP_DIM)
    cs_tab = jnp.asarray(np.concatenate([cc, sc], axis=1), BF16)
    c1, s1 = _cos_sin(n1)
    ang = 2.0 * np.pi * np.outer(np.arange(n2), np.arange(n1)) / seq
    ct, st = (jnp.asarray(v[:, :, None], F32) for v in (np.cos(ang), np.sin(ang)))
    c1, s1 = jnp.asarray(c1[None], F32), jnp.asarray(s1[None], F32)
    mc = c1 * ct - s1 * st
    ms = s1 * ct + c1 * st
    m_tab = jnp.concatenate([jnp.concatenate([mc, -ms], axis=2),
                             jnp.concatenate([ms, mc], axis=2)], axis=1).astype(BF16)
    c2, s2 = (jnp.asarray(v, BF16) for v in _cos_sin(n2))

    f_t = uvf[:seq, col0:col0 + d_fft].reshape(n1, n2, d_fft).transpose(1, 0, 2)
    ta = _pick(n2, (8, 4, 2, 1))
    z = pl.pallas_call(
        functools.partial(_fft_a_kernel, n1=n1),
        grid=(n2 // ta,),
        in_specs=[pl.BlockSpec((ta, n1, d_fft), lambda a: (a, 0, 0)),
                  pl.BlockSpec((ta, 2 * n1, 2 * n1), lambda a: (a, 0, 0)),
                  pl.BlockSpec((GROUP_DIM, 2 * GROUP_DIM), lambda a: (0, 0))],
        out_specs=pl.BlockSpec((ta, 2 * n1, d_fft), lambda a: (a, 0, 0)),
        out_shape=jax.ShapeDtypeStruct((n2, 2 * n1, d_fft), BF16),
        compiler_params=_params("parallel"),
        name="fft_stage_a",
    )(f_t, m_tab, cs_tab)

    width = n1 * d_fft
    tw = _pick(width, (8192, 4096, 2048, 1024, 512, 256, 128))
    nb = width // tw
    z2 = z.reshape(n2, 2 * width)
    tab = lambda: pl.BlockSpec((n2, n2), lambda j: (0, 0))
    y_x = pl.pallas_call(
        functools.partial(_fft_b_kernel, scale=float(1.0 / np.sqrt(seq * GROUP_DIM))),
        grid=(nb,),
        in_specs=[pl.BlockSpec((n2, tw), lambda j: (0, j)),
                  pl.BlockSpec((n2, tw), lambda j: (0, nb + j)),
                  tab(), tab()],
        out_specs=pl.BlockSpec((n2, tw), lambda j: (0, j)),
        out_shape=jax.ShapeDtypeStruct((n2, width), BF16),
        compiler_params=_params("parallel"),
        name="fft_stage_b",
    )(z2, z2, c2, s2).reshape(seq, d_fft)

    assert ctx_len == ROW_BLOCK and col0 % d_fft == 0
    cl, sl = (jnp.asarray(v, BF16) for v in _cos_sin(ctx_len))
    ctab = lambda: pl.BlockSpec((ctx_len, ctx_len), lambda j: (0, 0))
    y_c = pl.pallas_call(
        functools.partial(_fft_ctx_kernel, scale=float(1.0 / np.sqrt(ctx_len * GROUP_DIM))),
        grid=(1,),
        in_specs=[pl.BlockSpec((ctx_len, d_fft), lambda j: (seq // ctx_len, col0 // d_fft)),
                  ctab(), ctab(),
                  pl.BlockSpec((GROUP_DIM, 2 * GROUP_DIM), lambda j: (0, 0))],
        out_specs=pl.BlockSpec((ctx_len, d_fft), lambda j: (0, 0)),
        out_shape=jax.ShapeDtypeStruct((ctx_len, d_fft), BF16),
        compiler_params=_params("arbitrary"),
        name="fft_ctx",
    )(uvf, cl, sl, cs_tab)
    return jnp.concatenate([y_x, y_c], axis=0)


def kernel(x, c, ctx, c_ctx, ada_w1, ada_w2, ada_b, norm_ffn1, f1_w1, f1_w3, f1_w2, norm_mix, w_in, b_in,
           conv_w, conv_b, a_log, dt_bias, d_skip, norm_ssd, sgu_norm, sgu_w, sgu_b,
           w_br_ssd, w_br_mlp, w_br_fft, w_out, norm_ffn2, f2_w1, f2_w3, f2_w2, norm_final):
    batch, seq, d = x.shape
    ctx_len = ctx.shape[1]
    depth = ada_w1.shape[0]
    assert batch == 1 and seq % ROW_BLOCK == 0
    heads = a_log.shape[-1]
    d_ssd = norm_ssd.shape[-1]
    c_xbc = conv_w.shape[-1]
    d_state = (c_xbc - d_ssd) // (2 * SSD_GROUPS)
    d_mlp = sgu_norm.shape[-1]
    d_fft = w_br_fft.shape[1]
    assert heads * SSD_HEAD_DIM == d_ssd and d_mlp == d_fft
    off_xbc = d_ssd
    off_dt = off_xbc + c_xbc
    off_u = off_dt + 2 * heads
    off_gate = off_u + 2 * d_mlp + d_fft

    cv = jnp.zeros((8, d), F32).at[0].set(c[0]).at[1].set(c_ctx)
    mods = ada_mods(cv, ada_w1, ada_w2, ada_b)[:, 0:2, :].reshape(depth, 2, N_MOD, d)
    bf = lambda w: w.astype(BF16)
    f1_w1, f1_w3, f1_w2, f2_w1, f2_w3, f2_w2 = map(bf, (f1_w1, f1_w3, f1_w2, f2_w1, f2_w3, f2_w2))
    w_br_ssd, w_br_mlp, w_br_fft, w_out = map(bf, (w_br_ssd, w_br_mlp, w_br_fft, w_out))
    w_in_t = bf(jnp.swapaxes(w_in, 1, 2))
    b_dt = jnp.pad(b_in[:, off_dt:off_u], ((0, 0), (0, LANES - 2 * heads)))
    n_uvf = off_gate - off_u

    xc = xb = (x[0], ctx[0])
    for i in range(depth):
        sh1, sc1, g1, shm, scm, gm, sh2, sc2, g2 = (mods[i, :, k, :] for k in range(N_MOD))

        h = norm_mod(xb, norm_ffn1[i], sh1, sc1, seq)
        xc, xb = down_residual(swiglu_up(h, f1_w1, f1_w3, i), f1_w2, i, xc, g1, 0.5, seq, shadow=True)

        h = norm_mod(xb, norm_mix[i], shm, scm, seq)
        zx = proj(h, w_in_t, b_in[i], i, 0, off_dt, BF16, name="proj_zx")
        dt_raw = proj(h, w_in_t, b_dt[i], i, off_dt, LANES, F32, name="proj_dt")
        uvf = proj(h, w_in_t, b_in[i, off_u:], i, off_u, n_uvf, BF16, name="proj_uvf")
        gates = proj(h, w_in_t, b_in[i, off_gate:], i, off_gate, 3 * d, BF16, sigmoid=True, name="proj_gates")

        xbc = conv_silu(zx, off_xbc, conv_w[i], conv_b[i], seq, ctx_len)
        a_neg = -jnp.exp(a_log[i].astype(F32)).reshape(-1)
        y_ssd = ssd_mixer(xbc, zx, dt_raw, dt_bias[i].reshape(-1), a_neg,
                          jnp.repeat(d_skip[i], SSD_HEAD_DIM), norm_ssd[i], seq, d_ssd, d_state)
        y_mlp = sgu(uvf, sgu_norm[i], sgu_w[i], sgu_b[i])
        y_fft = fourier_mix(uvf, 2 * d_mlp, d_fft, seq, ctx_len)

        merged = merge(y_ssd, y_mlp, y_fft, w_br_ssd, w_br_mlp, w_br_fft, i, gates)
        xc, xb = down_residual(merged, w_out, i, xc, gm, 1.0, seq, shadow=True)

        h = norm_mod(xb, norm_ffn2[i], sh2, sc2, seq)
        up = swiglu_up(h, f2_w1, f2_w3, i)
        if i + 1 < depth:
            xc, xb = down_residual(up, f2_w2, i, xc, g2, 0.5, seq, shadow=True)
        else:
            xc = down_residual(up, f2_w2, i, xc, g2, 0.5, seq)

    return final_norm(xc, norm_final, seq).reshape(1, seq, d)
```

```python
import functools

import numpy as np
import jax
import jax.numpy as jnp
from jax import lax
from jax.experimental import pallas as pl
from jax.experimental.pallas import tpu as pltpu

F32 = jnp.float32
BF16 = jnp.bfloat16
HI = lax.Precision.HIGHEST

EPS = 1e-6
GRID_W = 64
SSD_GROUPS = 4
SSD_HEAD_DIM = 64
CHUNK = 128
D_CONV = 5
GROUP_DIM = 128
N_MOD = 9
LANES = 128
ROW_BLOCK = 256
VMEM_LIMIT = 56 * 1024 * 1024


def _pick(n, cands):
    for c in cands:
        if n % c == 0:
            return c
    raise ValueError(f"no tile for {n} in {cands}")


def _params(*sem):
    return pltpu.CompilerParams(dimension_semantics=sem, vmem_limit_bytes=VMEM_LIMIT)


def _silu(v):
    return v * jax.nn.sigmoid(v)


def _gelu_tanh(v):
    return v * (0.5 * (1.0 + jnp.tanh(np.sqrt(2.0 / np.pi).astype(np.float32) * (v + 0.044715 * (v * v * v)))))


def _dot(a, b, precision=None):
    return jnp.dot(a, b, preferred_element_type=F32, precision=precision)


def _ada_kernel(cv_ref, w1_ref, w2_ref, b_ref, o_ref):
    t = _dot(_silu(cv_ref[...]), w1_ref[0], HI)
    o_ref[0] = _dot(t, w2_ref[0], HI) + b_ref[0]


def ada_mods(cv, w1, w2, b):
    nl, d, r = w1.shape
    n = w2.shape[-1]
    tn = _pick(n, (4608, 4096, 2304, 2048, 1024, 512, 256, 128))
    return pl.pallas_call(
        _ada_kernel,
        grid=(nl, n // tn),
        in_specs=[pl.BlockSpec((8, d), lambda l, j: (0, 0)),
                  pl.BlockSpec((1, d, r), lambda l, j: (l, 0, 0)),
                  pl.BlockSpec((1, r, tn), lambda l, j: (l, 0, j)),
                  pl.BlockSpec((1, 1, tn), lambda l, j: (l, 0, j))],
        out_specs=pl.BlockSpec((1, 8, tn), lambda l, j: (l, 0, j)),
        out_shape=jax.ShapeDtypeStruct((nl, 8, n), F32),
        compiler_params=_params("arbitrary", "arbitrary"),
        name="ada_mods",
    )(cv, w1, w2, b.reshape(nl, 1, n))


def _norm_mod_kernel(*refs, nx, split):
    if split:
        x_ref, c_ref, gm_ref, sh_ref, o_ref = refs
    else:
        x_ref, gm_ref, sh_ref, o_ref = refs

    def emit(src_ref):
        x = src_ref[...].astype(F32)
        rstd = lax.rsqrt(jnp.mean(x * x, axis=-1, keepdims=True) + EPS)
        o_ref[...] = (x * rstd * gm_ref[0] + sh_ref[0]).astype(o_ref.dtype)

    if split:
        pl.when(pl.program_id(0) < nx)(lambda: emit(x_ref))
        pl.when(pl.program_id(0) >= nx)(lambda: emit(c_ref))
    else:
        emit(x_ref)


def norm_mod(xs, g, shift, scale, seq):
    split = isinstance(xs, tuple)
    d = g.shape[0]
    nx = seq // ROW_BLOCK
    sel = lambda i: (jnp.where(i >= nx, 1, 0), 0, 0)
    gmul = g[None, :] * (1.0 + scale)
    if split:
        t = seq + xs[1].shape[0]
        src_specs = [pl.BlockSpec((ROW_BLOCK, d), lambda i: (jnp.minimum(i, nx - 1), 0)),
                     pl.BlockSpec((ROW_BLOCK, d), lambda i: (jnp.maximum(i - nx, 0), 0))]
    else:
        xs = (xs,)
        t = xs[0].shape[0]
        src_specs = [pl.BlockSpec((ROW_BLOCK, d), lambda i: (i, 0))]
    return pl.pallas_call(
        functools.partial(_norm_mod_kernel, nx=nx, split=split),
        grid=(t // ROW_BLOCK,),
        in_specs=src_specs + [pl.BlockSpec((1, 1, d), sel), pl.BlockSpec((1, 1, d), sel)],
        out_specs=pl.BlockSpec((ROW_BLOCK, d), lambda i: (i, 0)),
        out_shape=jax.ShapeDtypeStruct((t, d), BF16),
        compiler_params=_params("parallel"),
        name="norm_mod",
    )(*xs, gmul.reshape(2, 1, d), shift.reshape(2, 1, d))


def _final_norm_kernel(x_ref, g_ref, o_ref):
    x = x_ref[...]
    o_ref[...] = x * lax.rsqrt(jnp.mean(x * x, axis=-1, keepdims=True) + EPS) * g_ref[...]


def final_norm(xc, g, seq):
    d = xc.shape[1]
    return pl.pallas_call(
        _final_norm_kernel,
        grid=(seq // ROW_BLOCK,),
        in_specs=[pl.BlockSpec((ROW_BLOCK, d), lambda i: (i, 0)),
                  pl.BlockSpec((1, d), lambda i: (0, 0))],
        out_specs=pl.BlockSpec((ROW_BLOCK, d), lambda i: (i, 0)),
        out_shape=jax.ShapeDtypeStruct((seq, d), F32),
        compiler_params=_params("parallel"),
        name="final_norm",
    )(xc, g.reshape(1, d))


def _mm_tiles(t, n, wide=False):
    tm = _pick(t, (1280, 1024, 768, 640, 512, 384, 256))
    tn = _pick(n, ((1024,) if wide else ()) + (512, 384, 256, 128))
    return tm, tn


def _w_spec(k, tn, layer, col0=0):
    return pl.BlockSpec((None, k, tn), lambda i, j: (layer, 0, col0 + j))


def _up_kernel(h_ref, w1_ref, w3_ref, o_ref):
    h = h_ref[...]
    a = _dot(h, w1_ref[...])
    o_ref[...] = (_silu(a) * _dot(h, w3_ref[...])).astype(o_ref.dtype)


def swiglu_up(h, w1, w3, layer):
    t, d = h.shape
    n = w1.shape[2]
    tm, tn = _mm_tiles(t, n)
    return pl.pallas_call(
        _up_kernel,
        grid=(t // tm, n // tn),
        in_specs=[pl.BlockSpec((tm, d), lambda i, j: (i, 0)),
                  _w_spec(d, tn, layer), _w_spec(d, tn, layer)],
        out_specs=pl.BlockSpec((tm, tn), lambda i, j: (i, j)),
        out_shape=jax.ShapeDtypeStruct((t, n), BF16),
        compiler_params=_params("parallel", "arbitrary"),
        name="swiglu_up",
    )(h, w1, w3)


def _down_kernel(*refs, coef, tm, seq, n_latent_last, shadow):
    split = n_latent_last is not None
    it = iter(refs)
    a_ref, w_ref, x_ref = next(it), next(it), next(it)
    c_ref = next(it) if split else None
    gate_ref, o_ref = next(it), next(it)
    s_ref = next(it) if shadow else None
    i = pl.program_id(0)
    rows = i * tm + lax.broadcasted_iota(jnp.int32, (tm, 1), 0)
    gate = jnp.where(rows >= seq, gate_ref[1:2, :], gate_ref[0:1, :])
    upd = (coef * gate) * _dot(a_ref[...], w_ref[...])

    def emit(new, rs):
        o_ref[rs, :] = new
        if shadow:
            s_ref[rs, :] = new.astype(s_ref.dtype)

    if not split:
        emit(x_ref[...] + upd, slice(None))
        return
    last = pl.num_programs(0) - 1

    @pl.when(i < last)
    def _():
        emit(x_ref[...] + upd, slice(None))

    @pl.when(i == last)
    def _():
        lat, cx = slice(0, n_latent_last), slice(n_latent_last, tm)
        emit(x_ref[lat, :] + upd[lat, :], lat)
        emit(c_ref[...] + upd[cx, :], cx)


def down_residual(a, w, layer, res, gate, coef, seq, shadow=False):
    t, k = a.shape
    n = w.shape[2]
    tm, tn = _mm_tiles(t, n)
    split = isinstance(res, tuple)
    tile = pl.BlockSpec((tm, tn), lambda i, j: (i, j))
    if split:
        ctx_len = res[1].shape[0]
        n_latent_last = tm - ctx_len
        assert res[0].shape[0] == seq and seq + ctx_len == t and 0 < n_latent_last and n_latent_last % 16 == 0
        res_specs = [tile, pl.BlockSpec((ctx_len, tn), lambda i, j: (0, j))]
    else:
        res, n_latent_last = (res,), None
        res_specs = [tile]
    out = pl.pallas_call(
        functools.partial(_down_kernel, coef=coef, tm=tm, seq=seq, n_latent_last=n_latent_last, shadow=shadow),
        grid=(t // tm, n // tn),
        in_specs=[pl.BlockSpec((tm, k), lambda i, j: (i, 0)), _w_spec(k, tn, layer)] + res_specs
                 + [pl.BlockSpec((2, tn), lambda i, j: (0, j))],
        out_specs=[tile] * (1 + shadow),
        out_shape=[jax.ShapeDtypeStruct((t, n), F32)] + [jax.ShapeDtypeStruct((t, n), BF16)] * shadow,
        input_output_aliases={} if split else {2: 0},
        compiler_params=_params("parallel", "arbitrary"),
        name="down_residual",
    )(a, w, *res, gate)
    return out if shadow else out[0]


def _proj_kernel(h_ref, wt_ref, b_ref, o_ref, *, sigmoid):
    acc = lax.dot_general(h_ref[...], wt_ref[...], (((1,), (1,)), ((), ())), preferred_element_type=F32) + b_ref[...]
    if sigmoid:
        acc = jax.nn.sigmoid(acc)
    o_ref[...] = acc.astype(o_ref.dtype)


def proj(h, wt, b, layer, col0, n, out_dtype, sigmoid=False, name="proj"):
    t, d = h.shape
    tm, tn = _mm_tiles(t, n, wide=True)
    nl, nw, _ = wt.shape
    row0 = layer * nw + col0
    assert tn % LANES == 0 and n % tn == 0 and row0 % 16 == 0
    return pl.pallas_call(
        functools.partial(_proj_kernel, sigmoid=sigmoid),
        grid=(t // tm, n // tn),
        in_specs=[pl.BlockSpec((tm, d), lambda i, j: (i, 0)),
                  pl.BlockSpec((pl.Element(tn), pl.Element(d)), lambda i, j: (pl.multiple_of(row0 + j * tn, 16), 0)),
                  pl.BlockSpec((1, tn), lambda i, j: (0, j))],
        out_specs=pl.BlockSpec((tm, tn), lambda i, j: (i, j)),
        out_shape=jax.ShapeDtypeStruct((t, n), out_dtype),
        compiler_params=_params("parallel", "arbitrary"),
        name=name,
    )(h, wt.reshape(nl * nw, d), b[:n].reshape(1, n))


def _merge_kernel(ys_ref, ym_ref, yf_ref, ws_ref, wm_ref, wf_ref, gs_ref, gm_ref, gf_ref, o_ref):
    acc = gs_ref[...].astype(F32) * _dot(ys_ref[...], ws_ref[...])
    acc = acc + gm_ref[...].astype(F32) * _dot(ym_ref[...], wm_ref[...])
    acc = acc + gf_ref[...].astype(F32) * _dot(yf_ref[...], wf_ref[...])
    o_ref[...] = acc.astype(o_ref.dtype)


def merge(y_ssd, y_mlp, y_fft, w_ssd, w_mlp, w_fft, layer, gates):
    t = y_ssd.shape[0]
    n = w_ssd.shape[2]
    tm, tn = _mm_tiles(t, n)
    nb = n // tn
    row = lambda k: pl.BlockSpec((tm, k), lambda i, j: (i, 0))
    col = lambda k: _w_spec(k, tn, layer)
    gate = lambda b: pl.BlockSpec((tm, tn), lambda i, j: (i, b * nb + j))
    return pl.pallas_call(
        _merge_kernel,
        grid=(t // tm, nb),
        in_specs=[row(y_ssd.shape[1]), row(y_mlp.shape[1]), row(y_fft.shape[1]),
                  col(w_ssd.shape[1]), col(w_mlp.shape[1]), col(w_fft.shape[1]),
                  gate(0), gate(1), gate(2)],
        out_specs=pl.BlockSpec((tm, tn), lambda i, j: (i, j)),
        out_shape=jax.ShapeDtypeStruct((t, n), BF16),
        compiler_params=_params("parallel", "arbitrary"),
        name="merge",
    )(y_ssd, y_mlp, y_fft, w_ssd, w_mlp, w_fft, gates, gates, gates)


def _conv_kernel(u_ref, s_ref, w_ref, b_ref, o_ref):
    u = u_ref[...]
    y = b_ref[...] + u.astype(F32) * w_ref[D_CONV // 2:D_CONV // 2 + 1, :]
    taps = [k for k in range(D_CONV) if k != D_CONV // 2]
    for m, k in enumerate(taps):
        y = y + _dot(s_ref[m], u) * w_ref[k:k + 1, :]
    o_ref[...] = _silu(y).astype(o_ref.dtype)


def _shift_matrices(rows, width):
    t = np.arange(rows)
    out = []
    for k in range(D_CONV):
        off = k - D_CONV // 2
        if off == 0:
            continue
        ok = ((t % width) + off >= 0) & ((t % width) + off < width)
        m = np.zeros((rows, rows), np.float32)
        m[t[ok], t[ok] + off] = 1.0
        out.append(m)
    return np.stack(out)


def conv_silu(zx, col0, conv_w, conv_b, seq, ctx_len):
    t = zx.shape[0]
    c = conv_w.shape[1]
    tc = next(v for v in (1024, 512, 256, 128) if c % v == 0 and col0 % v == 0)
    assert ctx_len == ROW_BLOCK and ROW_BLOCK % GRID_W == 0 and zx.dtype == BF16
    nx = seq // ROW_BLOCK
    shifts = jnp.asarray(np.stack([_shift_matrices(ROW_BLOCK, GRID_W), _shift_matrices(ROW_BLOCK, ctx_len)]), BF16)
    return pl.pallas_call(
        _conv_kernel,
        grid=(t // ROW_BLOCK, c // tc),
        in_specs=[pl.BlockSpec((ROW_BLOCK, tc), lambda i, j: (i, col0 // tc + j)),
                  pl.BlockSpec((None, D_CONV - 1, ROW_BLOCK, ROW_BLOCK),
                               lambda i, j: (jnp.where(i >= nx, 1, 0), 0, 0, 0)),
                  pl.BlockSpec((D_CONV, tc), lambda i, j: (0, j)),
                  pl.BlockSpec((1, tc), lambda i, j: (0, j))],
        out_specs=pl.BlockSpec((ROW_BLOCK, tc), lambda i, j: (i, j)),
        out_shape=jax.ShapeDtypeStruct((t, c), BF16),
        compiler_params=_params("parallel", "arbitrary"),
        name="conv_silu",
    )(zx, shifts, conv_w, conv_b.reshape(1, c))


LOG2E = float(np.log2(np.e))


def _decay_kernel(raw_ref, bias_ref, a_ref, cs_ref, src_ref, wst_ref, etot_ref, *, heads, chunks):
    qi = lax.broadcasted_iota(jnp.int32, (CHUNK, CHUNK), 0)
    si = lax.broadcasted_iota(jnp.int32, (CHUNK, CHUNK), 1)
    lower, upper = (si <= qi).astype(F32), (si >= qi).astype(F32)
    fwd_col = lax.broadcasted_iota(jnp.int32, (CHUNK, LANES), 1) < heads
    for c in range(chunks):
        rows = slice(c * CHUNK, (c + 1) * CHUNK)
        v = raw_ref[rows, :] + bias_ref[...]
        dt = jnp.maximum(v, 0.0) + jnp.log1p(jnp.exp(-jnp.abs(v)))
        a = dt * a_ref[...]
        pre, suf = _dot(lower, a, HI), _dot(upper, a, HI)
        cs = jnp.where(fwd_col, pre, suf)
        tot = jnp.where(fwd_col[0:1, :], pre[CHUNK - 1:CHUNK, :], suf[0:1, :])
        cs_ref[rows, :] = cs * LOG2E
        src_ref[rows, :] = ((cs - jnp.log(dt)) * LOG2E).T
        wst_ref[rows, :] = (jnp.exp(tot - cs) * dt).T
        etot_ref[c * 8:(c + 1) * 8, :] = jnp.broadcast_to(jnp.exp(tot), (8, LANES))


def ssd_decays(dt_raw, dt_bias, a_neg):
    t = dt_raw.shape[0]
    nc = t // CHUNK
    heads = dt_bias.shape[0] // 2
    chunks = _pick(nc, (10, 6, 5, 4, 3, 2, 1))
    pad = lambda vec: jnp.pad(vec.reshape(1, -1), ((0, 0), (0, LANES - vec.shape[0])))
    big = pl.BlockSpec((chunks * CHUNK, LANES), lambda i: (i, 0))
    vec = pl.BlockSpec((1, LANES), lambda i: (0, 0))
    full = jax.ShapeDtypeStruct((t, LANES), F32)
    return pl.pallas_call(
        functools.partial(_decay_kernel, heads=heads, chunks=chunks),
        grid=(nc // chunks,),
        in_specs=[big, vec, vec],
        out_specs=[big, big, big, pl.BlockSpec((chunks * 8, LANES), lambda i: (i, 0))],
        out_shape=[full, full, full, jax.ShapeDtypeStruct((nc * 8, LANES), F32)],
        compiler_params=_params("parallel"),
        name="ssd_decays",
    )(dt_raw, pad(dt_bias), pad(a_neg))


def _ssd_kernel(*refs, direction, heads, d_ssd, d_state, final):
    if final:
        x_ref, cs_ref, src_ref, wst_ref, etot_ref, yf_ref, z_ref, dsk_ref, g_ref, o_ref, state_ref = refs
    else:
        x_ref, cs_ref, src_ref, wst_ref, etot_ref, o_ref, state_ref = refs
    gp = d_ssd // SSD_GROUPS
    r_heads = gp // SSD_HEAD_DIM
    gn = SSD_GROUPS * d_state

    @pl.when(pl.program_id(0) == 0)
    def _():
        state_ref[...] = jnp.zeros_like(state_ref)

    qi = lax.broadcasted_iota(jnp.int32, (CHUNK, CHUNK), 0)
    si = lax.broadcasted_iota(jnp.int32, (CHUNK, CHUNK), 1)
    causal = (si <= qi) if direction == 0 else (si >= qi)
    cs = cs_ref[...]
    src_t = src_ref[...]
    wst_t = wst_ref[...]
    etot = etot_ref[0:1, :]
    lo = lax.broadcasted_iota(jnp.int32, (CHUNK, LANES), 1) < SSD_HEAD_DIM
    lo_row = lo[0:1, :]

    for g in range(SSD_GROUPS):
        h0 = direction * heads + g * r_heads
        bm = x_ref[:, d_ssd + g * d_state:d_ssd + (g + 1) * d_state]
        cm = x_ref[:, d_ssd + gn + g * d_state:d_ssd + gn + (g + 1) * d_state]
        cb = lax.dot_general(cm, bm, (((1,), (1,)), ((), ())), preferred_element_type=F32)
        bt = bm.astype(F32).T
        y_state = _dot(cm, state_ref[:, g * gp:(g + 1) * gp].astype(BF16))
        ys = []
        for k in range(r_heads // 2):
            ps = slice(g * gp + k * LANES, g * gp + (k + 1) * LANES)
            xd = x_ref[:, ps].astype(F32)
            y_acc = st_acc = ecs = None
            for half in range(2):
                h = h0 + 2 * k + half
                csq = jnp.broadcast_to(cs[:, h:h + 1], (CHUNK, CHUNK))
                m = (cb * jnp.exp2(jnp.where(causal, csq - src_t[h:h + 1, :], -1e30))).astype(BF16)
                xh = (jnp.where(lo, xd, 0.0) if half == 0 else jnp.where(lo, 0.0, xd)).astype(BF16)
                btw = (bt * wst_t[h:h + 1, :]).astype(BF16)
                py, pst = _dot(m, xh), _dot(btw, xh)
                e = jnp.exp2(csq)
                if half == 0:
                    y_acc, st_acc, ecs = py, pst, e
                else:
                    y_acc, st_acc, ecs = y_acc + py, st_acc + pst, jnp.where(lo, ecs, e)
            ys.append(y_acc + ecs * y_state[:, k * LANES:(k + 1) * LANES])
            h = h0 + 2 * k
            decay = jnp.where(lo_row, jnp.broadcast_to(etot[:, h:h + 1], (1, LANES)),
                              jnp.broadcast_to(etot[:, h + 1:h + 2], (1, LANES)))
            state_ref[:, ps] = state_ref[:, ps] * decay + st_acc

        gs = slice(g * gp, (g + 1) * gp)
        y = ys[0] if len(ys) == 1 else jnp.concatenate(ys, axis=1)
        if final:
            y = (yf_ref[:, gs] + y + dsk_ref[:, gs] * x_ref[:, gs].astype(F32)) * _silu(z_ref[:, gs].astype(F32))
            y = y * lax.rsqrt(jnp.mean(y * y, axis=-1, keepdims=True) + EPS) * g_ref[:, gs]
        o_ref[:, gs] = y.astype(o_ref.dtype)


def ssd_mixer(xbc, zx, dt_raw, dt_bias, a_neg, d_skip_x, norm_g, seq, d_ssd, d_state):
    t, c_xbc = xbc.shape
    nc = t // CHUNK
    nxc = seq // CHUNK
    ncc = nc - nxc
    heads = dt_bias.shape[0] // 2
    gp = d_ssd // SSD_GROUPS
    assert d_state == LANES and (gp // SSD_HEAD_DIM) % 2 == 0 and gp % LANES == 0 and 2 * heads <= LANES
    fwd = lambda j: jnp.where(j < ncc, nxc + j, j - ncc)
    bwd = lambda j: nc - 1 - j
    vec = lambda n: pl.BlockSpec((1, n), lambda j: (0, 0))
    decays = ssd_decays(dt_raw, dt_bias, a_neg)

    def call(direction, order, final, extra_in, extra_specs, out_dtype):
        per_chunk = pl.BlockSpec((CHUNK, LANES), lambda j: (order(j), 0))
        return pl.pallas_call(
            functools.partial(_ssd_kernel, direction=direction, heads=heads, d_ssd=d_ssd, d_state=d_state,
                              final=final),
            grid=(nc,),
            in_specs=[pl.BlockSpec((CHUNK, c_xbc), lambda j: (order(j), 0)),
                      per_chunk, per_chunk, per_chunk,
                      pl.BlockSpec((8, LANES), lambda j: (order(j), 0))] + extra_specs,
            out_specs=pl.BlockSpec((CHUNK, d_ssd), lambda j: (order(j), 0)),
            out_shape=jax.ShapeDtypeStruct((t, d_ssd), out_dtype),
            scratch_shapes=[pltpu.VMEM((d_state, d_ssd), F32)],
            compiler_params=_params("arbitrary"),
            name="ssd_bwd_out" if final else "ssd_fwd",
        )(xbc, *decays, *extra_in)

    y_f = call(0, fwd, False, [], [], F32)
    blk = pl.BlockSpec((CHUNK, d_ssd), lambda j: (bwd(j), 0))
    return call(1, bwd, True,
                [y_f, zx, d_skip_x.reshape(1, d_ssd), norm_g.reshape(1, d_ssd)],
                [blk, blk, vec(d_ssd), vec(d_ssd)], BF16)


def _sgu_kernel(u_ref, v_ref, g_ref, w_ref, bt_ref, o_ref, *, groups):
    for c in range(u_ref.shape[0] // CHUNK):
        rows = slice(c * CHUNK, (c + 1) * CHUNK)
        u = _gelu_tanh(u_ref[rows, :].astype(F32))
        v = _gelu_tanh(v_ref[rows, :].astype(F32))
        v = (v * lax.rsqrt(jnp.mean(v * v, axis=-1, keepdims=True) + EPS) * g_ref[...]).astype(BF16)
        for g in range(groups):
            sl = slice(g * GROUP_DIM, (g + 1) * GROUP_DIM)
            vm = _dot(w_ref[g], v[:, sl]) + bt_ref[:, g:g + 1]
            o_ref[rows, sl] = (u[:, sl] * vm).astype(o_ref.dtype)


def sgu(uvf, norm_g, w_s, b_s):
    t = uvf.shape[0]
    groups = w_s.shape[0]
    dm = groups * GROUP_DIM
    rb = CHUNK * _pick(t // CHUNK, (10, 5, 2, 1))
    return pl.pallas_call(
        functools.partial(_sgu_kernel, groups=groups),
        grid=(t // rb,),
        in_specs=[pl.BlockSpec((rb, dm), lambda c: (c, 0)),
                  pl.BlockSpec((rb, dm), lambda c: (c, 1)),
                  pl.BlockSpec((1, dm), lambda c: (0, 0)),
                  pl.BlockSpec((groups, CHUNK, CHUNK), lambda c: (0, 0, 0)),
                  pl.BlockSpec((CHUNK, groups), lambda c: (0, 0))],
        out_specs=pl.BlockSpec((rb, dm), lambda c: (c, 0)),
        out_shape=jax.ShapeDtypeStruct((t, dm), BF16),
        compiler_params=_params("parallel"),
        name="sgu",
    )(uvf, uvf, norm_g.reshape(1, dm), w_s.astype(BF16), b_s.T)


def _channel_dft(f, cs_tab):
    gr, gs = [], []
    for g in range(f.shape[1] // GROUP_DIM):
        p = _dot(f[:, g * GROUP_DIM:(g + 1) * GROUP_DIM], cs_tab)
        gr.append(p[:, :GROUP_DIM])
        gs.append(p[:, GROUP_DIM:])
    cat = lambda v: (v[0] if len(v) == 1 else jnp.concatenate(v, axis=1)).astype(BF16)
    return cat(gr), cat(gs)


def _fft_a_kernel(f_ref, m_ref, cs_ref, z_ref, *, n1):
    for a in range(f_ref.shape[0]):
        gr, gs = _channel_dft(f_ref[a], cs_ref[...])
        m = m_ref[a]
        z_ref[a] = (_dot(m[:, :n1], gr) + _dot(m[:, n1:], gs)).astype(z_ref.dtype)


def _fft_b_kernel(zr_ref, zi_ref, c_ref, s_ref, o_ref, *, scale):
    acc = _dot(c_ref[...], zr_ref[...]) - _dot(s_ref[...], zi_ref[...])
    o_ref[...] = (acc * scale).astype(o_ref.dtype)


def _fft_ctx_kernel(f_ref, c_ref, s_ref, cs_ref, o_ref, *, scale):
    gr, gs = _channel_dft(f_ref[...], cs_ref[...])
    acc = _dot(c_ref[...], gr) - _dot(s_ref[...], gs)
    o_ref[...] = (acc * scale).astype(o_ref.dtype)


def _cos_sin(n):
    ang = 2.0 * np.pi * (np.outer(np.arange(n), np.arange(n)) % n) / n
    return np.cos(ang), np.sin(ang)


def fourier_mix(uvf, col0, d_fft, seq, ctx_len):
    n2 = CHUNK
    n1 = seq // n2
    assert n1 * n2 == seq and n1 % 16 == 0 and d_fft % GROUP_DIM == 0
    cc, sc = _cos_sin(GROUP_DIM)
    cs_tab = jnp.asarray(np.concatenate([cc, sc], axis=1), BF16)
    c1, s1 = _cos_sin(n1)
    ang = 2.0 * np.pi * np.outer(np.arange(n2), np.arange(n1)) / seq
    ct, st = (jnp.asarray(v[:, :, None], F32) for v in (np.cos(ang), np.sin(ang)))
    c1, s1 = jnp.asarray(c1[None], F32), jnp.asarray(s1[None], F32)
    mc = c1 * ct - s1 * st
    ms = s1 * ct + c1 * st
    m_tab = jnp.concatenate([jnp.concatenate([mc, -ms], axis=2),
                             jnp.concatenate([ms, mc], axis=2)], axis=1).astype(BF16)
    c2, s2 = (jnp.asarray(v, BF16) for v in _cos_sin(n2))

    f_t = uvf[:seq, col0:col0 + d_fft].reshape(n1, n2, d_fft).transpose(1, 0, 2)
    ta = _pick(n2, (8, 4, 2, 1))
    z = pl.pallas_call(
        functools.partial(_fft_a_kernel, n1=n1),
        grid=(n2 // ta,),
        in_specs=[pl.BlockSpec((ta, n1, d_fft), lambda a: (a, 0, 0)),
                  pl.BlockSpec((ta, 2 * n1, 2 * n1), lambda a: (a, 0, 0)),
                  pl.BlockSpec((GROUP_DIM, 2 * GROUP_DIM), lambda a: (0, 0))],
        out_specs=pl.BlockSpec((ta, 2 * n1, d_fft), lambda a: (a, 0, 0)),
        out_shape=jax.ShapeDtypeStruct((n2, 2 * n1, d_fft), BF16),
        compiler_params=_params("parallel"),
        name="fft_stage_a",
    )(f_t, m_tab, cs_tab)

    width = n1 * d_fft
    tw = _pick(width, (8192, 4096, 2048, 1024, 512, 256, 128))
    nb = width // tw
    z2 = z.reshape(n2, 2 * width)
    tab = lambda: pl.BlockSpec((n2, n2), lambda j: (0, 0))
    y_x = pl.pallas_call(
        functools.partial(_fft_b_kernel, scale=float(1.0 / np.sqrt(seq * GROUP_DIM))),
        grid=(nb,),
        in_specs=[pl.BlockSpec((n2, tw), lambda j: (0, j)),
                  pl.BlockSpec((n2, tw), lambda j: (0, nb + j)),
                  tab(), tab()],
        out_specs=pl.BlockSpec((n2, tw), lambda j: (0, j)),
        out_shape=jax.ShapeDtypeStruct((n2, width), BF16),
        compiler_params=_params("parallel"),
        name="fft_stage_b",
    )(z2, z2, c2, s2).reshape(seq, d_fft)

    assert ctx_len == ROW_BLOCK and col0 % d_fft == 0
    cl, sl = (jnp.asarray(v, BF16) for v in _cos_sin(ctx_len))
    ctab = lambda: pl.BlockSpec((ctx_len, ctx_len), lambda j: (0, 0))
    y_c = pl.pallas_call(
        functools.partial(_fft_ctx_kernel, scale=float(1.0 / np.sqrt(ctx_len * GROUP_DIM))),
        grid=(1,),
        in_specs=[pl.BlockSpec((ctx_len, d_fft), lambda j: (seq // ctx_len, col0 // d_fft)),
                  ctab(), ctab(),
                  pl.BlockSpec((GROUP_DIM, 2 * GROUP_DIM), lambda j: (0, 0))],
        out_specs=pl.BlockSpec((ctx_len, d_fft), lambda j: (0, 0)),
        out_shape=jax.ShapeDtypeStruct((ctx_len, d_fft), BF16),
        compiler_params=_params("arbitrary"),
        name="fft_ctx",
    )(uvf, cl, sl, cs_tab)
    return jnp.concatenate([y_x, y_c], axis=0)


def kernel(x, c, ctx, c_ctx, ada_w1, ada_w2, ada_b, norm_ffn1, f1_w1, f1_w3, f1_w2, norm_mix, w_in, b_in,
           conv_w, conv_b, a_log, dt_bias, d_skip, norm_ssd, sgu_norm, sgu_w, sgu_b,
           w_br_ssd, w_br_mlp, w_br_fft, w_out, norm_ffn2, f2_w1, f2_w3, f2_w2, norm_final):
    batch, seq, d = x.shape
    ctx_len = ctx.shape[1]
    depth = ada_w1.shape[0]
    assert batch == 1 and seq % ROW_BLOCK == 0
    heads = a_log.shape[-1]
    d_ssd = norm_ssd.shape[-1]
    c_xbc = conv_w.shape[-1]
    d_state = (c_xbc - d_ssd) // (2 * SSD_GROUPS)
    d_mlp = sgu_norm.shape[-1]
    d_fft = w_br_fft.shape[1]
    assert heads * SSD_HEAD_DIM == d_ssd and d_mlp == d_fft
    off_xbc = d_ssd
    off_dt = off_xbc + c_xbc
    off_u = off_dt + 2 * heads
    off_gate = off_u + 2 * d_mlp + d_fft

    cv = jnp.zeros((8, d), F32).at[0].set(c[0]).at[1].set(c_ctx)
    mods = ada_mods(cv, ada_w1, ada_w2, ada_b)[:, 0:2, :].reshape(depth, 2, N_MOD, d)
    bf = lambda w: w.astype(BF16)
    f1_w1, f1_w3, f1_w2, f2_w1, f2_w3, f2_w2 = map(bf, (f1_w1, f1_w3, f1_w2, f2_w1, f2_w3, f2_w2))
    w_br_ssd, w_br_mlp, w_br_fft, w_out = map(bf, (w_br_ssd, w_br_mlp, w_br_fft, w_out))
    w_in_t = bf(jnp.swapaxes(w_in, 1, 2))
    b_dt = jnp.pad(b_in[:, off_dt:off_u], ((0, 0), (0, LANES - 2 * heads)))
    n_uvf = off_gate - off_u

    xc = xb = (x[0], ctx[0])
    for i in range(depth):
        sh1, sc1, g1, shm, scm, gm, sh2, sc2, g2 = (mods[i, :, k, :] for k in range(N_MOD))

        h = norm_mod(xb, norm_ffn1[i], sh1, sc1, seq)
        xc, xb = down_residual(swiglu_up(h, f1_w1, f1_w3, i), f1_w2, i, xc, g1, 0.5, seq, shadow=True)

        h = norm_mod(xb, norm_mix[i], shm, scm, seq)
        zx = proj(h, w_in_t, b_in[i], i, 0, off_dt, BF16, name="proj_zx")
        dt_raw = proj(h, w_in_t, b_dt[i], i, off_dt, LANES, F32, name="proj_dt")
        uvf = proj(h, w_in_t, b_in[i, off_u:], i, off_u, n_uvf, BF16, name="proj_uvf")
        gates = proj(h, w_in_t, b_in[i, off_gate:], i, off_gate, 3 * d, BF16, sigmoid=True, name="proj_gates")

        xbc = conv_silu(zx, off_xbc, conv_w[i], conv_b[i], seq, ctx_len)
        a_neg = -jnp.exp(a_log[i].astype(F32)).reshape(-1)
        y_ssd = ssd_mixer(xbc, zx, dt_raw, dt_bias[i].reshape(-1), a_neg,
                          jnp.repeat(d_skip[i], SSD_HEAD_DIM), norm_ssd[i], seq, d_ssd, d_state)
        y_mlp = sgu(uvf, sgu_norm[i], sgu_w[i], sgu_b[i])
        y_fft = fourier_mix(uvf, 2 * d_mlp, d_fft, seq, ctx_len)

        merged = merge(y_ssd, y_mlp, y_fft, w_br_ssd, w_br_mlp, w_br_fft, i, gates)
        xc, xb = down_residual(merged, w_out, i, xc, gm, 1.0, seq, shadow=True)

        h = norm_mod(xb, norm_ffn2[i], sh2, sc2, seq)
        up = swiglu_up(h, f2_w1, f2_w3, i)
        if i + 1 < depth:
            xc, xb = down_residual(up, f2_w2, i, xc, g2, 0.5, seq, shadow=True)
        else:
            xc = down_residual(up, f2_w2, i, xc, g2, 0.5, seq)

    return final_norm(xc, norm_final, seq).reshape(1, seq, d)
```
